```python
import math
import jax, jax.numpy as jnp
from jax import lax
import numpy as np

D_MODEL = 1024
BATCH = 4
SEQ = 4096
DEPTH = 2
DEC_BATCH = 8
DEC_SEQ = 4096
PAST_LEN = 128

N_HEADS = 8
HEAD_DIM = 64
V_HEAD_DIM = 2 * HEAD_DIM
ROT_DIM = HEAD_DIM // 4
ROPE_THETA = 500000.0
Q_BLOCK = 128
ATTN_QK_WIDTH = N_HEADS * 2 * HEAD_DIM
ATTN_V_WIDTH = N_HEADS * V_HEAD_DIM
CHUNK = 128
GMLP_GROUPS = 8
GMLP_GROUP_DIM = 128
GMLP_WIDTH = GMLP_GROUPS * GMLP_GROUP_DIM
N_BRANCH = 2
Q_OFF = 0
K_OFF = Q_OFF + ATTN_QK_WIDTH
V_OFF = K_OFF + ATTN_QK_WIDTH
U_OFF = V_OFF + ATTN_V_WIDTH
VG_OFF = U_OFF + GMLP_WIDTH
GATE_OFF = VG_OFF + GMLP_WIDTH
IN_WIDTH = GATE_OFF + N_BRANCH * D_MODEL
N_EXPERTS = 16
EXPERT_FF = 2048
CAPACITY_FACTOR = 2
N_MOD = 6
EPS = 1e-6

kernel_name = "hybrid_diffattn_gmlp_ec_moe_encoder"


def rmsnorm(x, g):
    xf = x.astype(jnp.float32)
    r = xf * lax.rsqrt(jnp.mean(xf * xf, axis=-1, keepdims=True) + EPS)
    return (r * g.astype(jnp.float32)).astype(x.dtype)


def layernorm(x, g, b):
    xf = x.astype(jnp.float32)
    mu = jnp.mean(xf, axis=-1, keepdims=True)
    xc = xf - mu
    var = jnp.mean(xc * xc, axis=-1, keepdims=True)
    return (xc * lax.rsqrt(var + EPS) * g.astype(jnp.float32) + b.astype(jnp.float32)).astype(x.dtype)


def modulate(x, shift, scale):
    return x * (1.0 + scale[:, None, :]) + shift[:, None, :]


def rope_tables(S):
    inv = ROPE_THETA ** (-jnp.arange(0, ROT_DIM, 2, dtype=jnp.float32) / ROT_DIM)
    ang = jnp.arange(S, dtype=jnp.float32)[:, None] * inv[None, :]
    return jnp.cos(ang), jnp.sin(ang)


def apply_rope(t, cos, sin):
    cos = cos[None, :, None, None, :].astype(t.dtype)
    sin = sin[None, :, None, None, :].astype(t.dtype)
    half = ROT_DIM // 2
    t1 = t[..., :half]
    t2 = t[..., half:ROT_DIM]
    return jnp.concatenate([t1 * cos - t2 * sin, t2 * cos + t1 * sin, t[..., ROT_DIM:]], axis=-1)


def diff_attention(q, k, v, lam, lam_init, subln_g):
    B, S = q.shape[0], q.shape[1]
    nb = S // Q_BLOCK
    scale = HEAD_DIM ** -0.5
    kt = k.transpose(0, 2, 3, 1, 4)
    vt = v.transpose(0, 2, 1, 3)
    qb = q.reshape(B, nb, Q_BLOCK, N_HEADS, 2, HEAD_DIM).transpose(1, 0, 3, 4, 2, 5)

    def block(qblk):
        s = jnp.einsum('bhmqd,bhmkd->bhmqk', qblk, kt).astype(jnp.float32) * scale
        p = jax.nn.softmax(s, axis=-1)
        pd = p[:, :, 0] - lam * p[:, :, 1]
        return jnp.einsum('bhqk,bhkd->bhqd', pd.astype(vt.dtype), vt)

    o = lax.map(block, qb)
    o = o.transpose(1, 0, 3, 2, 4).reshape(B, S, N_HEADS, V_HEAD_DIM)
    o = rmsnorm(o, subln_g) * (1.0 - lam_init)
    return o.reshape(B, S, ATTN_V_WIDTH)


def spatial_gating(u, vg, ln_g, ln_b, w_s, b_s):
    B, S = vg.shape[0], vg.shape[1]
    nc = S // CHUNK
    vn = layernorm(vg, ln_g, ln_b).reshape(B, nc, CHUNK, GMLP_GROUPS, GMLP_GROUP_DIM)
    mixed = jnp.einsum('gpq,bnqgc->bnpgc', w_s, vn) + b_s.T[None, None, :, :, None]
    return u * mixed.reshape(B, S, GMLP_WIDTH)


def expert_choice_moe(h, w_router, w_gate, w_up, w_down):
    B, S, D = h.shape
    n = B * S
    cap = CAPACITY_FACTOR * n // N_EXPERTS
    hf = h.reshape(n, D)
    aff = jax.nn.softmax((hf @ w_router).astype(jnp.float32), axis=-1)
    g, idx = lax.top_k(aff.T, cap)
    xe = hf[idx]
    hid = jax.nn.silu(jnp.einsum('ecd,edf->ecf', xe, w_gate)) * jnp.einsum('ecd,edf->ecf', xe, w_up)
    ye = jnp.einsum('ecf,efd->ecd', hid, w_down) * g[..., None].astype(h.dtype)
    out = jnp.zeros_like(hf).at[idx.reshape(-1)].add(ye.reshape(-1, D))
    return out.reshape(B, S, D)


def trunk(x, c, norm1_g, w_ada, b_ada, w_in, b_gate, lam, subln_g, gmlp_ln_g, gmlp_ln_b,
          w_spatial, b_spatial, w_br_attn, w_br_gmlp, w_out, norm2_g, w_router,
          w_e_gate, w_e_up, w_e_down, final_g):
    B, S, _ = x.shape
    cos, sin = rope_tables(S)
    for l in range(DEPTH):
        mod = jax.nn.silu(c) @ w_ada[l] + b_ada[l]
        sh1, sc1, gt1, sh2, sc2, gt2 = jnp.split(mod, N_MOD, axis=-1)
        h = modulate(rmsnorm(x, norm1_g[l]), sh1, sc1)
        proj = h @ w_in[l]
        q = proj[..., Q_OFF:K_OFF].reshape(B, S, N_HEADS, 2, HEAD_DIM)
        k = proj[..., K_OFF:V_OFF].reshape(B, S, N_HEADS, 2, HEAD_DIM)
        v = proj[..., V_OFF:U_OFF].reshape(B, S, N_HEADS, V_HEAD_DIM)
        u = jax.nn.gelu(proj[..., U_OFF:VG_OFF])
        vg = jax.nn.gelu(proj[..., VG_OFF:GATE_OFF])
        gates = jax.nn.sigmoid(proj[..., GATE_OFF:] + b_gate[l])
        g_attn, g_gmlp = jnp.split(gates, N_BRANCH, axis=-1)
        q = apply_rope(q, cos, sin)
        k = apply_rope(k, cos, sin)
        lam_init = 0.8 - 0.6 * math.exp(-0.3 * l)
        lp = lam[l].astype(jnp.float32)
        lam_full = jnp.exp(jnp.sum(lp[0] * lp[1])) - jnp.exp(jnp.sum(lp[2] * lp[3])) + lam_init
        o_attn = diff_attention(q, k, v, lam_full, lam_init, subln_g[l]) @ w_br_attn[l]
        o_gmlp = spatial_gating(u, vg, gmlp_ln_g[l], gmlp_ln_b[l], w_spatial[l], b_spatial[l]) @ w_br_gmlp[l]
        mixed = (g_attn * o_attn + g_gmlp * o_gmlp) @ w_out[l]
        x = x + gt1[:, None, :] * mixed
        h2 = modulate(rmsnorm(x, norm2_g[l]), sh2, sc2)
        x = x + gt2[:, None, :] * expert_choice_moe(h2, w_router[l], w_e_gate[l], w_e_up[l], w_e_down[l])
    return rmsnorm(x, final_g)


def setup_inputs(seed: int = 0) -> dict:
    key = jax.random.key(seed)
    ks = jax.random.split(key, 32)
    f = jnp.float32
    D = D_MODEL

    def nrm(k, shape, scale):
        return jax.random.normal(k, shape, f) * scale

    return {
        "x_prompt": nrm(ks[0], (BATCH, SEQ, D), 1.0),
        "x_sample": nrm(ks[1], (DEC_BATCH, DEC_SEQ, D), 1.0),
        "c_prompt": nrm(ks[2], (BATCH, D), 1.0),
        "c_sample": nrm(ks[3], (DEC_BATCH, D), 1.0),
        "norm1_g": 1.0 + nrm(ks[4], (DEPTH, D), 0.01),
        "w_ada": nrm(ks[5], (DEPTH, D, N_MOD * D), 0.5 * D ** -0.5),
        "b_ada": nrm(ks[6], (DEPTH, N_MOD * D), 0.01),
        "w_in": nrm(ks[7], (DEPTH, D, IN_WIDTH), D ** -0.5),
        "b_gate": nrm(ks[8], (DEPTH, N_BRANCH * D), 0.01),
        "lam": nrm(ks[9], (DEPTH, 4, HEAD_DIM), 0.1),
        "subln_g": 1.0 + nrm(ks[10], (DEPTH, V_HEAD_DIM), 0.01),
        "gmlp_ln_g": 1.0 + nrm(ks[11], (DEPTH, GMLP_WIDTH), 0.01),
        "gmlp_ln_b": nrm(ks[12], (DEPTH, GMLP_WIDTH), 0.01),
        "w_spatial": nrm(ks[13], (DEPTH, GMLP_GROUPS, CHUNK, CHUNK), CHUNK ** -0.5),
        "b_spatial": 1.0 + nrm(ks[14], (DEPTH, GMLP_GROUPS, CHUNK), 0.01),
        "w_br_attn": nrm(ks[15], (DEPTH, ATTN_V_WIDTH, D), ATTN_V_WIDTH ** -0.5),
        "w_br_gmlp": nrm(ks[16], (DEPTH, GMLP_WIDTH, D), GMLP_WIDTH ** -0.5),
        "w_out": nrm(ks[17], (DEPTH, D, D), D ** -0.5),
        "norm2_g": 1.0 + nrm(ks[18], (DEPTH, D), 0.01),
        "w_router": nrm(ks[19], (DEPTH, D, N_EXPERTS), D ** -0.5),
        "w_e_gate": nrm(ks[20], (DEPTH, N_EXPERTS, D, EXPERT_FF), D ** -0.5),
        "w_e_up": nrm(ks[21], (DEPTH, N_EXPERTS, D, EXPERT_FF), D ** -0.5),
        "w_e_down": nrm(ks[22], (DEPTH, N_EXPERTS, EXPERT_FF, D), EXPERT_FF ** -0.5),
        "final_g": 1.0 + nrm(ks[23], (D,), 0.01),
    }


def reference(x_prompt, x_sample, c_prompt, c_sample, norm1_g, w_ada, b_ada, w_in, b_gate, lam,
              subln_g, gmlp_ln_g, gmlp_ln_b, w_spatial, b_spatial, w_br_attn, w_br_gmlp, w_out,
              norm2_g, w_router, w_e_gate, w_e_up, w_e_down, final_g):
    y_prompt = trunk(x_prompt, c_prompt, norm1_g, w_ada, b_ada, w_in, b_gate, lam, subln_g,
                     gmlp_ln_g, gmlp_ln_b, w_spatial, b_spatial, w_br_attn, w_br_gmlp, w_out,
                     norm2_g, w_router, w_e_gate, w_e_up, w_e_down, final_g)
    y_sample = trunk(x_sample, c_sample, norm1_g, w_ada, b_ada, w_in, b_gate, lam, subln_g,
                     gmlp_ln_g, gmlp_ln_b, w_spatial, b_spatial, w_br_attn, w_br_gmlp, w_out,
                     norm2_g, w_router, w_e_gate, w_e_up, w_e_down, final_g)
    return (y_prompt, y_sample)
```

```python
import functools
import math

import jax
import jax.numpy as jnp
from jax import lax
from jax.experimental import pallas as pl
from jax.experimental.pallas import tpu as pltpu

F32 = jnp.float32
BF16 = jnp.bfloat16
I32 = jnp.int32

D_MODEL = 1024
SEQ = 4096
DEPTH = 2
N_HEADS = 8
HEAD_DIM = 64
ROT_DIM = HEAD_DIM // 4
ROPE_THETA = 500000.0
CHUNK = 128
GMLP_GROUPS = 8
N_EXPERTS = 16
EXPERT_FF = 2048
CAPACITY_FACTOR = 2
N_MOD = 6
N_SEG = 7
EPS = 1e-6

LANES = 128
VMEM_LIMIT = 56 * 1024 * 1024

TM_PROJ = 512
TQ = 256
TM_MIX = 256
TB = 256
SLAB = 64
MAXCH = TB // SLAB
PAD = TB


def _cparams(sem):
    return pltpu.CompilerParams(dimension_semantics=sem, vmem_limit_bytes=VMEM_LIMIT)


def _ada_kernel(c_ref, w_ref, b_ref, o_ref):
    c = c_ref[...]
    a = c * jax.nn.sigmoid(c)
    o_ref[0] = jnp.dot(a, w_ref[0], preferred_element_type=F32,
                       precision=lax.Precision.HIGHEST) + b_ref[0]


def _ada(c_pad, w_ada, b_ada):
    bp, d = c_pad.shape
    depth = w_ada.shape[0]
    return pl.pallas_call(
        _ada_kernel,
        grid=(depth, N_MOD),
        in_specs=[
            pl.BlockSpec((bp, d), lambda l, j: (0, 0)),
            pl.BlockSpec((1, d, d), lambda l, j: (l, 0, j)),
            pl.BlockSpec((1, 1, d), lambda l, j: (l, 0, j)),
        ],
        out_specs=pl.BlockSpec((1, bp, d), lambda l, j: (l, 0, j)),
        out_shape=jax.ShapeDtypeStruct((depth, bp, N_MOD * d), F32),
        compiler_params=_cparams(("arbitrary", "arbitrary")),
        name="ada_mod",
    )(c_pad, w_ada, b_ada.reshape(depth, 1, N_MOD * d))


def _gelu(x):
    return jax.nn.gelu(x)


def _inproj_kernel(x_ref, sh_ref, sc_ref, g_ref, w_ref, bg_ref, cos_ref, sa_ref, sb_ref,
                   o_ref, h_ref):
    j = pl.program_id(1)

    @pl.when(j == 0)
    def _():
        x = x_ref[...]
        r = x * lax.rsqrt(jnp.mean(x * x, axis=-1, keepdims=True) + EPS) * g_ref[...]
        h_ref[...] = (r * (1.0 + sc_ref[...]) + sh_ref[...]).astype(BF16)

    acc = jnp.dot(h_ref[...], w_ref[...], preferred_element_type=F32)

    @pl.when(j < 2)
    def _():
        scale = jnp.where(j == 0, HEAD_DIM ** -0.5, 1.0).astype(F32)
        cos, sa, sb = cos_ref[...], sa_ref[...], sb_ref[...]
        half = ROT_DIM // 2
        parts = []
        for c in range(acc.shape[1] // LANES):
            a = acc[:, c * LANES:(c + 1) * LANES]
            parts.append(a * cos + pltpu.roll(a, half, 1) * sa + pltpu.roll(a, LANES - half, 1) * sb)
        o_ref[...] = (jnp.concatenate(parts, axis=-1) * scale).astype(BF16)

    @pl.when(j == 2)
    def _():
        o_ref[...] = acc.astype(BF16)

    @pl.when(jnp.logical_or(j == 3, j == 4))
    def _():
        o_ref[...] = _gelu(acc).astype(BF16)

    @pl.when(j >= 5)
    def _():
        o_ref[...] = jax.nn.sigmoid(acc + bg_ref[...]).astype(BF16)


def _inproj(x, mod3, norm_g, w_in, b_gate, cos_t, sa_t, sb_t):
    n, d = x.shape
    tm = min(TM_PROJ, SEQ)
    tiles_per_seq = SEQ // tm
    return pl.pallas_call(
        _inproj_kernel,
        grid=(n // tm, N_SEG),
        in_specs=[
            pl.BlockSpec((tm, d), lambda i, j: (i, 0)),
            pl.BlockSpec((None, 1, d), lambda i, j: ((i // tiles_per_seq) * N_MOD + 0, 0, 0)),
            pl.BlockSpec((None, 1, d), lambda i, j: ((i // tiles_per_seq) * N_MOD + 1, 0, 0)),
            pl.BlockSpec((1, d), lambda i, j: (0, 0)),
            pl.BlockSpec((d, d), lambda i, j: (0, j)),
            pl.BlockSpec((1, d), lambda i, j: (0, jnp.maximum(j - 5, 0))),
            pl.BlockSpec((tm, LANES), lambda i, j: (i % tiles_per_seq, 0)),
            pl.BlockSpec((tm, LANES), lambda i, j: (i % tiles_per_seq, 0)),
            pl.BlockSpec((tm, LANES), lambda i, j: (i % tiles_per_seq, 0)),
        ],
        out_specs=pl.BlockSpec((tm, d), lambda i, j: (i, j)),
        out_shape=jax.ShapeDtypeStruct((n, N_SEG * d), BF16),
        scratch_shapes=[pltpu.VMEM((tm, d), BF16)],
        compiler_params=_cparams(("arbitrary", "arbitrary")),
        name="in_proj",
    )(x, mod3, mod3, norm_g, w_in, b_gate, cos_t, sa_t, sb_t)


def _attn_kernel(q_ref, k_ref, v_ref, lam_ref, sg_ref, o_ref, *, lam_init):
    q = q_ref[...]
    k = k_ref[...]
    v = v_ref[...]
    lane = lax.broadcasted_iota(I32, q.shape, 1)
    zero = jnp.zeros_like(q)
    dn = (((1,), (1,)), ((), ()))

    def one_map(qm):
        s = lax.dot_general(qm, k, dn, preferred_element_type=F32)
        m = jnp.max(s, axis=-1, keepdims=True)
        p = jnp.exp(s - m)
        l = jnp.sum(p, axis=-1, keepdims=True)
        return jnp.dot(p.astype(BF16), v, preferred_element_type=F32) / l

    o0 = one_map(jnp.where(lane < HEAD_DIM, q, zero))
    o1 = one_map(jnp.where(lane >= HEAD_DIM, q, zero))
    lp = lam_ref[...]
    lam = (jnp.exp(jnp.sum(lp[0:1] * lp[1:2], axis=-1, keepdims=True))
           - jnp.exp(jnp.sum(lp[2:3] * lp[3:4], axis=-1, keepdims=True)) + lam_init)
    o = o0 - lam * o1
    r = o * lax.rsqrt(jnp.mean(o * o, axis=-1, keepdims=True) + EPS)
    o_ref[...] = ((r * sg_ref[...]) * (1.0 - lam_init)).astype(BF16)


def _attention(proj3, lam, subln_g, lam_init):
    bt, s, _ = proj3.shape
    vd = 2 * HEAD_DIM
    tq = min(TQ, s)
    return pl.pallas_call(
        functools.partial(_attn_kernel, lam_init=lam_init),
        grid=(bt, N_HEADS, s // tq),
        in_specs=[
            pl.BlockSpec((None, tq, vd), lambda b, h, i: (b, i, h)),
            pl.BlockSpec((None, s, vd), lambda b, h, i: (b, 0, N_HEADS + h)),
            pl.BlockSpec((None, s, vd), lambda b, h, i: (b, 0, 2 * N_HEADS + h)),
            pl.BlockSpec((4, HEAD_DIM), lambda b, h, i: (0, 0)),
            pl.BlockSpec((1, vd), lambda b, h, i: (0, 0)),
        ],
        out_specs=pl.BlockSpec((None, tq, vd), lambda b, h, i: (b, i, h)),
        out_shape=jax.ShapeDtypeStruct((bt, s, N_HEADS * vd), BF16),
        compiler_params=_cparams(("arbitrary", "arbitrary", "arbitrary")),
        name="diff_attn",
    )(proj3, proj3, proj3, lam, subln_g)


def _mix_kernel(u_ref, vg_ref, ga_ref, gg_ref, at_ref, x_ref, lng_ref, lnb_ref, ws_ref, bs_ref,
                wba_ref, wbg_ref, wo_ref, gt1_ref, n2g_ref, sh2_ref, sc2_ref, wrh_ref, wrl_ref,
                x1_ref, h2_ref, aff_ref):
    tm = u_ref.shape[0]
    vg = vg_ref[...].astype(F32)
    mu = jnp.mean(vg, axis=-1, keepdims=True)
    xc = vg - mu
    var = jnp.mean(xc * xc, axis=-1, keepdims=True)
    vn = (xc * lax.rsqrt(var + EPS) * lng_ref[...] + lnb_ref[...]).astype(BF16)
    bs = bs_ref[...]
    cols = []
    for g in range(GMLP_GROUPS):
        rows = []
        for c in range(tm // CHUNK):
            blk = vn[c * CHUNK:(c + 1) * CHUNK, g * LANES:(g + 1) * LANES]
            rows.append(jnp.dot(ws_ref[g], blk, preferred_element_type=F32) + bs[:, g:g + 1])
        cols.append(jnp.concatenate(rows, axis=0))
    mixed = jnp.concatenate(cols, axis=1)
    gm = (u_ref[...].astype(F32) * mixed).astype(BF16)
    o_g = jnp.dot(gm, wbg_ref[...], preferred_element_type=F32)
    o_a = jnp.dot(at_ref[...], wba_ref[...], preferred_element_type=F32)
    merged = (ga_ref[...].astype(F32) * o_a + gg_ref[...].astype(F32) * o_g).astype(BF16)
    mo = jnp.dot(merged, wo_ref[...], preferred_element_type=F32)
    x1 = x_ref[...] + gt1_ref[...] * mo
    x1_ref[...] = x1
    r = x1 * lax.rsqrt(jnp.mean(x1 * x1, axis=-1, keepdims=True) + EPS) * n2g_ref[...]
    h2 = r * (1.0 + sc2_ref[...]) + sh2_ref[...]
    h2h = h2.astype(BF16)
    h2_ref[...] = h2h
    h2l = (h2 - h2h.astype(F32)).astype(BF16)
    wrh = wrh_ref[...]
    logits = (jnp.dot(h2h, wrh, preferred_element_type=F32)
              + jnp.dot(h2l, wrh, preferred_element_type=F32)
              + jnp.dot(h2h, wrl_ref[...], preferred_element_type=F32))
    m = jnp.max(logits, axis=-1, keepdims=True)
    p = jnp.exp(logits - m)
    aff_ref[...] = p / jnp.sum(p, axis=-1, keepdims=True)


def _mix(proj, attn, x, mod3, lng, lnb, ws, bs_t, wba, wbg, wo, n2g, wrh, wrl):
    n, d = x.shape
    tm = min(TM_MIX, SEQ)
    tps = SEQ // tm
    row = lambda i: (i, 0)
    full2 = lambda i: (0, 0)

    def seg(k):
        return pl.BlockSpec((tm, d), lambda i: (i, k))

    def modspec(k):
        return pl.BlockSpec((None, 1, d), lambda i: ((i // tps) * N_MOD + k, 0, 0))

    return pl.pallas_call(
        _mix_kernel,
        grid=(n // tm,),
        in_specs=[
            seg(3), seg(4), seg(5), seg(6),
            pl.BlockSpec((tm, d), row),
            pl.BlockSpec((tm, d), row),
            pl.BlockSpec((1, d), full2),
            pl.BlockSpec((1, d), full2),
            pl.BlockSpec((GMLP_GROUPS, CHUNK, CHUNK), lambda i: (0, 0, 0)),
            pl.BlockSpec((CHUNK, GMLP_GROUPS), full2),
            pl.BlockSpec((d, d), full2),
            pl.BlockSpec((d, d), full2),
            pl.BlockSpec((d, d), full2),
            modspec(2),
            pl.BlockSpec((1, d), full2),
            modspec(3),
            modspec(4),
            pl.BlockSpec((d, N_EXPERTS), full2),
            pl.BlockSpec((d, N_EXPERTS), full2),
        ],
        out_specs=[
            pl.BlockSpec((tm, d), row),
            pl.BlockSpec((tm, d), row),
            pl.BlockSpec((tm, N_EXPERTS), row),
        ],
        out_shape=[
            jax.ShapeDtypeStruct((n, d), F32),
            jax.ShapeDtypeStruct((n, d), BF16),
            jax.ShapeDtypeStruct((n, N_EXPERTS), F32),
        ],
        compiler_params=_cparams(("arbitrary",)),
        name="branch_mix",
    )(proj, proj, proj, proj, attn, x, lng, lnb, ws, bs_t, wba, wbg, wo, mod3, n2g, mod3, mod3,
      wrh, wrl)


def _route_kernel(bits_ref, pos_ref, s0_ref, cnt_ref, *, groups):
    ne, tb = bits_ref.shape[1], bits_ref.shape[2]
    ri = lax.broadcasted_iota(I32, (tb, tb), 0)
    ci = lax.broadcasted_iota(I32, (tb, tb), 1)
    tri = jnp.where(ri < ci, 1.0, 0.0).astype(BF16)
    blk_lane = lax.broadcasted_iota(I32, s0_ref.shape, 1)
    s0_ref[...] = jnp.zeros(s0_ref.shape, I32)
    cnt_ref[...] = jnp.zeros(cnt_ref.shape, I32)

    for (b0, nb, cap) in groups:
        def count(pred_fn):
            def body(i, acc):
                return acc + jnp.where(pred_fn(bits_ref[b0 + i]), 1, 0)
            acc = lax.fori_loop(0, nb, body, jnp.zeros((ne, tb), I32))
            return jnp.sum(acc, axis=1, keepdims=True)

        def bs_body(k, thr):
            cand = thr | lax.shift_left(jnp.int32(1), jnp.int32(30) - k)
            c = count(lambda blk: blk >= cand)
            return jnp.where(c >= cap, cand, thr)

        thr = lax.fori_loop(0, 31, bs_body, jnp.zeros((ne, 1), I32))
        n_gt = count(lambda blk: blk > thr)
        need = (cap - n_gt).astype(F32)

        def scan_body(i, carry):
            ceq, cpos = carry
            blk = bits_ref[b0 + i]
            gt = blk > thr
            eq = blk == thr
            eqf = jnp.where(eq, 1.0, 0.0)
            eq_excl = jnp.dot(eqf.astype(BF16), tri, preferred_element_type=F32) + ceq
            self = jnp.where(gt, 1.0, jnp.where(eq_excl < need, eqf, 0.0))
            pos_excl = jnp.dot(self.astype(BF16), tri, preferred_element_type=F32) + cpos
            pos_ref[b0 + i] = jnp.where(self > 0.5, pos_excl.astype(I32), -1)
            n_sel = jnp.sum(self, axis=1, keepdims=True)
            hit = blk_lane == (b0 + i)
            s0_ref[...] = jnp.where(hit, cpos.astype(I32), s0_ref[...])
            cnt_ref[...] = jnp.where(hit, n_sel.astype(I32), cnt_ref[...])
            return (ceq + jnp.sum(eqf, axis=1, keepdims=True), cpos + n_sel)

        zero = jnp.zeros((ne, 1), F32)
        lax.fori_loop(0, nb, scan_body, (zero, zero))


def _route(bits3, groups):
    nblk, ne, tb = bits3.shape
    nbp = -(-nblk // LANES) * LANES
    return pl.pallas_call(
        functools.partial(_route_kernel, groups=groups),
        out_shape=[
            jax.ShapeDtypeStruct((nblk, ne, tb), I32),
            jax.ShapeDtypeStruct((ne, nbp), I32),
            jax.ShapeDtypeStruct((ne, nbp), I32),
        ],
        compiler_params=pltpu.CompilerParams(vmem_limit_bytes=VMEM_LIMIT),
        name="route_topc",
    )(bits3)


def _slot_base(b, e, layout):
    nb0, cp0, cp_tot = layout
    return e * cp_tot + jnp.where(b >= nb0, cp0, 0)


def _dispatch_kernel(s0_ref, nch_ref, h_ref, pos_ref, xe_ref, slab_ref, zero_ref, sems, *,
                     layout, pad_rows):
    b = pl.program_id(0)
    h = h_ref[...]
    tb, d = h.shape
    nsub = d // LANES
    jrow = lax.broadcasted_iota(I32, (SLAB, tb), 0)

    def copy(e, c, row0):
        return pltpu.make_async_copy(slab_ref.at[e, c], xe_ref.at[pl.ds(row0 + c * SLAB, SLAB)],
                                     sems.at[e, c])

    for e in range(N_EXPERTS):
        s0 = s0_ref[b * N_EXPERTS + e]
        rel = pos_ref[e:e + 1, :] - s0
        row0 = _slot_base(b, e, layout) + s0

        def chunk(c, carry, e=e, rel=rel, row0=row0):
            pt = jnp.where(jrow + c * SLAB == rel, 1.0, 0.0).astype(BF16)
            y = jnp.dot(pt, h, preferred_element_type=F32)
            for k in range(nsub):
                slab_ref[e, c, :, k, :] = y[:, k * LANES:(k + 1) * LANES]
            copy(e, c, row0).start()
            return carry

        lax.fori_loop(0, nch_ref[b * N_EXPERTS + e], chunk, 0)

    for e in range(N_EXPERTS):
        row0 = _slot_base(b, e, layout) + s0_ref[b * N_EXPERTS + e]

        def drain(c, carry, e=e, row0=row0):
            copy(e, c, row0).wait()
            return carry

        lax.fori_loop(0, nch_ref[b * N_EXPERTS + e], drain, 0)

    for (last_b, first_pad) in pad_rows:
        @pl.when(b == last_b)
        def _(last_b=last_b, first_pad=first_pad):
            zero_ref[...] = jnp.zeros(zero_ref.shape, F32)
            cps = []
            for e in range(N_EXPERTS):
                for c in range(PAD // SLAB):
                    r0 = _slot_base(last_b, e, layout) + first_pad + c * SLAB
                    cp = pltpu.make_async_copy(zero_ref, xe_ref.at[pl.ds(r0, SLAB)], sems.at[e, c])
                    cp.start()
                    cps.append(cp)
            for cp in cps:
                cp.wait()


def _dispatch(s0, nch, h2, pos3, layout, pad_rows, rows):
    n, d = h2.shape
    nsub = d // LANES
    grid_spec = pltpu.PrefetchScalarGridSpec(
        num_scalar_prefetch=2,
        grid=(n // TB,),
        in_specs=[
            pl.BlockSpec((TB, d), lambda b, s0, nch: (b, 0)),
            pl.BlockSpec((None, N_EXPERTS, TB), lambda b, s0, nch: (b, 0, 0)),
        ],
        out_specs=pl.BlockSpec(memory_space=pl.ANY),
        scratch_shapes=[
            pltpu.VMEM((N_EXPERTS, MAXCH, SLAB, nsub, LANES), F32),
            pltpu.VMEM((SLAB, nsub, LANES), F32),
            pltpu.SemaphoreType.DMA((N_EXPERTS, MAXCH)),
        ],
    )
    return pl.pallas_call(
        functools.partial(_dispatch_kernel, layout=layout, pad_rows=pad_rows),
        grid_spec=grid_spec,
        out_shape=jax.ShapeDtypeStruct((rows, nsub, LANES), F32),
        compiler_params=_cparams(("arbitrary",)),
        name="moe_dispatch",
    )(s0, nch, h2, pos3)


def _ffn_kernel(x_ref, wg_ref, wu_ref, wd_ref, y_ref):
    nsub = x_ref.shape[1]
    x = jnp.concatenate([x_ref[:, k, :] for k in range(nsub)], axis=-1).astype(BF16)
    g = jnp.dot(x, wg_ref[...], preferred_element_type=F32)
    u = jnp.dot(x, wu_ref[...], preferred_element_type=F32)
    hid = (g * jax.nn.sigmoid(g) * u).astype(BF16)
    y = jnp.dot(hid, wd_ref[...], preferred_element_type=F32)
    for k in range(nsub):
        y_ref[:, k, :] = y[:, k * LANES:(k + 1) * LANES]


def _ffn_tile(cp_tot):
    for tm in (512, 256, 128, 64):
        if cp_tot % tm == 0:
            return tm
    raise ValueError("slot rows per expert must be a multiple of 64")


def _ffn(xe3, wg, wu, wd, cp_tot):
    rows, nsub, _ = xe3.shape
    d = nsub * LANES
    ff = wg.shape[-1]
    tm = _ffn_tile(cp_tot)
    tiles = cp_tot // tm
    return pl.pallas_call(
        _ffn_kernel,
        grid=(N_EXPERTS, tiles),
        in_specs=[
            pl.BlockSpec((tm, nsub, LANES), lambda e, r: (e * tiles + r, 0, 0)),
            pl.BlockSpec((None, d, ff), lambda e, r: (e, 0, 0), pipeline_mode=pl.Buffered(1)),
            pl.BlockSpec((None, d, ff), lambda e, r: (e, 0, 0), pipeline_mode=pl.Buffered(1)),
            pl.BlockSpec((None, ff, d), lambda e, r: (e, 0, 0), pipeline_mode=pl.Buffered(1)),
        ],
        out_specs=pl.BlockSpec((tm, nsub, LANES), lambda e, r: (e * tiles + r, 0, 0)),
        out_shape=jax.ShapeDtypeStruct((rows, nsub, LANES), F32),
        compiler_params=_cparams(("arbitrary", "arbitrary")),
        name="expert_ffn",
    )(xe3, wg, wu, wd)


def _combine_kernel(s0_ref, nch_ref, ye_ref, pos_ref, aff_ref, x_ref, gt_ref, fg_ref, o_ref,
                    slab_ref, acc_ref, sems, *, layout, final):
    b = pl.program_id(0)
    tb, d = x_ref.shape
    nsub = d // LANES
    jrow = lax.broadcasted_iota(I32, (SLAB, tb), 0)
    dn = (((0,), (0,)), ((), ()))

    def copy(e, c, row0):
        return pltpu.make_async_copy(ye_ref.at[pl.ds(row0 + c * SLAB, SLAB)], slab_ref.at[e, c],
                                     sems.at[e, c])

    for e in range(N_EXPERTS):
        row0 = _slot_base(b, e, layout) + s0_ref[b * N_EXPERTS + e]

        def fetch(c, carry, e=e, row0=row0):
            copy(e, c, row0).start()
            return carry

        lax.fori_loop(0, nch_ref[b * N_EXPERTS + e], fetch, 0)

    acc_ref[...] = jnp.zeros(acc_ref.shape, F32)
    aff = aff_ref[...]
    for e in range(N_EXPERTS):
        s0 = s0_ref[b * N_EXPERTS + e]
        rel = pos_ref[e:e + 1, :] - s0
        row0 = _slot_base(b, e, layout) + s0
        w = aff[:, e:e + 1]

        def chunk(c, carry, e=e, rel=rel, row0=row0, w=w):
            copy(e, c, row0).wait()
            y = jnp.concatenate([slab_ref[e, c, :, k, :] for k in range(nsub)], axis=-1)
            pt = jnp.where(jrow + c * SLAB == rel, 1.0, 0.0).astype(BF16)
            got = lax.dot_general(pt, y.astype(BF16), dn, preferred_element_type=F32)
            acc_ref[...] += got * w
            return carry

        lax.fori_loop(0, nch_ref[b * N_EXPERTS + e], chunk, 0)

    x2 = x_ref[...] + gt_ref[...] * acc_ref[...]
    if final:
        x2 = x2 * lax.rsqrt(jnp.mean(x2 * x2, axis=-1, keepdims=True) + EPS) * fg_ref[...]
    o_ref[...] = x2


def _combine(s0, nch, ye3, pos3, aff, x1, mod3, final_g, layout, final):
    n, d = x1.shape
    nsub = d // LANES
    tps = SEQ // TB
    grid_spec = pltpu.PrefetchScalarGridSpec(
        num_scalar_prefetch=2,
        grid=(n // TB,),
        in_specs=[
            pl.BlockSpec(memory_space=pl.ANY),
            pl.BlockSpec((None, N_EXPERTS, TB), lambda b, s0, nch: (b, 0, 0)),
            pl.BlockSpec((TB, N_EXPERTS), lambda b, s0, nch: (b, 0)),
            pl.BlockSpec((TB, d), lambda b, s0, nch: (b, 0)),
            pl.BlockSpec((None, 1, d), lambda b, s0, nch: ((b // tps) * N_MOD + 5, 0, 0)),
            pl.BlockSpec((1, d), lambda b, s0, nch: (0, 0)),
        ],
        out_specs=pl.BlockSpec((TB, d), lambda b, s0, nch: (b, 0)),
        scratch_shapes=[
            pltpu.VMEM((N_EXPERTS, MAXCH, SLAB, nsub, LANES), F32),
            pltpu.VMEM((TB, d), F32),
            pltpu.SemaphoreType.DMA((N_EXPERTS, MAXCH)),
        ],
    )
    return pl.pallas_call(
        functools.partial(_combine_kernel, layout=layout, final=final),
        grid_spec=grid_spec,
        out_shape=jax.ShapeDtypeStruct((n, d), F32),
        compiler_params=_cparams(("arbitrary",)),
        name="moe_combine",
    )(s0, nch, ye3, pos3, aff, x1, mod3, final_g)


def _rope_tables():
    half = ROT_DIM // 2
    inv = ROPE_THETA ** (-jnp.arange(0, ROT_DIM, 2, dtype=F32) / ROT_DIM)
    ang = jnp.arange(SEQ, dtype=F32)[:, None] * inv[None, :]
    cos, sin = jnp.cos(ang), jnp.sin(ang)
    dd = jnp.arange(LANES) % HEAD_DIM
    cos_l = jnp.where(dd[None, :] < ROT_DIM, cos[:, dd % half], 1.0)
    sin_l = sin[:, dd % half]
    sa = jnp.where((dd[None, :] >= half) & (dd[None, :] < ROT_DIM), sin_l, 0.0)
    sb = jnp.where(dd[None, :] < half, -sin_l, 0.0)
    return cos_l.astype(F32), sa.astype(F32), sb.astype(F32)


def _trunk(xs, cs, norm1_g, w_ada, b_ada, w_in, b_gate, lam, subln_g, gmlp_ln_g, gmlp_ln_b,
           w_spatial, b_spatial, w_br_attn, w_br_gmlp, w_out, norm2_g, w_router,
           w_e_gate, w_e_up, w_e_down, final_g):
    d = D_MODEL
    assert N_HEADS * 2 * HEAD_DIM == d and GMLP_GROUPS * LANES == d and CHUNK == LANES
    assert SEQ % TB == 0 and TB % CHUNK == 0
    batches = [x.shape[0] for x in xs]
    bt = sum(batches)
    n = bt * SEQ
    x = jnp.concatenate([xx.reshape(-1, d) for xx in xs], axis=0)
    c = jnp.concatenate(cs, axis=0)
    bp = -(-bt // 8) * 8
    c_pad = jnp.pad(c, ((0, bp - bt), (0, 0)))
    mod = _ada(c_pad, w_ada, b_ada)

    caps = [CAPACITY_FACTOR * b * SEQ // N_EXPERTS for b in batches]
    nbs = [b * SEQ // TB for b in batches]
    cps = [cap + PAD for cap in caps]
    cp_tot = sum(cps)
    layout = (nbs[0], cps[0], cp_tot)
    groups = ((0, nbs[0], caps[0]), (nbs[0], nbs[1], caps[1]))
    pad_rows = ((nbs[0] - 1, caps[0]), (nbs[0] + nbs[1] - 1, caps[1]))
    rows = N_EXPERTS * cp_tot
    nblk = n // TB

    cos_t, sa_t, sb_t = _rope_tables()
    y = None
    for l in range(DEPTH):
        lam_init = 0.8 - 0.6 * math.exp(-0.3 * l)
        mod3 = mod[l, :bt].reshape(bt * N_MOD, 1, d)
        proj = _inproj(x, mod3, norm1_g[l][None], w_in[l].astype(BF16), b_gate[l][None],
                       cos_t, sa_t, sb_t)
        attn = _attention(proj.reshape(bt, SEQ, N_SEG * d), lam[l], subln_g[l][None], lam_init)
        wr = w_router[l]
        wrh = wr.astype(BF16)
        wrl = (wr - wrh.astype(F32)).astype(BF16)
        x1, h2, aff = _mix(proj, attn.reshape(n, d), x, mod3, gmlp_ln_g[l][None],
                           gmlp_ln_b[l][None], w_spatial[l].astype(BF16), b_spatial[l].T,
                           w_br_attn[l].astype(BF16), w_br_gmlp[l].astype(BF16),
                           w_out[l].astype(BF16), norm2_g[l][None], wrh, wrl)
        bits3 = lax.bitcast_convert_type(aff, I32).reshape(nblk, TB, N_EXPERTS).transpose(0, 2, 1)
        pos3, s0_t, cnt_t = _route(bits3, groups)
        s0 = s0_t[:, :nblk].T.reshape(-1)
        nch = (cnt_t[:, :nblk].T.reshape(-1) + (SLAB - 1)) // SLAB
        xe3 = _dispatch(s0, nch, h2, pos3, layout, pad_rows, rows)
        ye3 = _ffn(xe3, w_e_gate[l].astype(BF16), w_e_up[l].astype(BF16),
                   w_e_down[l].astype(BF16), cp_tot)
        final = l == DEPTH - 1
        x = _combine(s0, nch, ye3, pos3, aff, x1, mod3, final_g[None], layout, final)
    outs = []
    off = 0
    for b in batches:
        outs.append(x[off:off + b * SEQ].reshape(b, SEQ, d))
        off += b * SEQ
    return tuple(outs)


def kernel(x_prompt, x_sample, c_prompt, c_sample, norm1_g, w_ada, b_ada, w_in, b_gate, lam, subln_g, gmlp_ln_g, gmlp_ln_b, w_spatial, b_spatial, w_br_attn, w_br_gmlp, w_out, norm2_g, w_router, w_e_gate, w_e_up, w_e_down, final_g):
    return _trunk((x_prompt, x_sample), (c_prompt, c_sample), norm1_g, w_ada, b_ada, w_in,
                  b_gate, lam, subln_g, gmlp_ln_g, gmlp_ln_b, w_spatial, b_spatial, w_br_attn,
                  w_br_gmlp, w_out, norm2_g, w_router, w_e_gate, w_e_up, w_e_down, final_g)
```

```python
import functools
import math

import jax
import jax.numpy as jnp
from jax import lax
from jax.experimental import pallas as pl
from jax.experimental.pallas import tpu as pltpu

F32 = jnp.float32
BF16 = jnp.bfloat16
I32 = jnp.int32

D_MODEL = 1024
SEQ = 4096
DEPTH = 2
N_HEADS = 8
HEAD_DIM = 64
ROT_DIM = HEAD_DIM // 4
ROPE_THETA = 500000.0
CHUNK = 128
GMLP_GROUPS = 8
N_EXPERTS = 16
EXPERT_FF = 2048
CAPACITY_FACTOR = 2
N_MOD = 6
N_SEG = 7
EPS = 1e-6

LANES = 128
VMEM_LIMIT = 56 * 1024 * 1024

TM_PROJ = 512
NC_PROJ = 256
TQ = 256
TK = 512
TM_MIX = 256
TB = 256
SLAB = 64
ROWS_BF16 = 16
MAXCH = -(-(TB + ROWS_BF16 - 1) // SLAB)
PAD = TB


def _cparams(sem):
    return pltpu.CompilerParams(dimension_semantics=sem, vmem_limit_bytes=VMEM_LIMIT)


def _ada_kernel(c_ref, w_ref, b_ref, o_ref):
    c = c_ref[...]
    a = c * jax.nn.sigmoid(c)
    o_ref[0] = jnp.dot(a, w_ref[0], preferred_element_type=F32,
                       precision=lax.Precision.HIGHEST) + b_ref[0]


def _ada(c_pad, w_ada, b_ada):
    bp, d = c_pad.shape
    depth = w_ada.shape[0]
    return pl.pallas_call(
        _ada_kernel,
        grid=(depth, N_MOD),
        in_specs=[
            pl.BlockSpec((bp, d), lambda l, j: (0, 0)),
            pl.BlockSpec((1, d, d), lambda l, j: (l, 0, j)),
            pl.BlockSpec((1, 1, d), lambda l, j: (l, 0, j)),
        ],
        out_specs=pl.BlockSpec((1, bp, d), lambda l, j: (l, 0, j)),
        out_shape=jax.ShapeDtypeStruct((depth, bp, N_MOD * d), F32),
        compiler_params=_cparams(("arbitrary", "arbitrary")),
        name="ada_mod",
    )(c_pad, w_ada, b_ada.reshape(depth, 1, N_MOD * d))


def _gelu(x):
    return jax.nn.gelu(x)


def _inproj_kernel(x_ref, sh_ref, sc_ref, g_ref, w_ref, bg_ref, cos_ref, sa_ref, sb_ref,
                   o_ref, h_ref):
    j = pl.program_id(1)

    @pl.when(j == 0)
    def _():
        x = x_ref[...]
        r = x * lax.rsqrt(jnp.mean(x * x, axis=-1, keepdims=True) + EPS) * g_ref[...]
        h_ref[...] = (r * (1.0 + sc_ref[...]) + sh_ref[...]).astype(BF16)

    def project(epilogue):
        h = h_ref[...]
        for c in range(w_ref.shape[1] // NC_PROJ):
            cols = slice(c * NC_PROJ, (c + 1) * NC_PROJ)
            acc = jnp.dot(h, w_ref[:, cols], preferred_element_type=F32)
            o_ref[:, cols] = epilogue(acc, cols).astype(BF16)

    @pl.when(j < 2)
    def _():
        scale = jnp.where(j == 0, HEAD_DIM ** -0.5 * math.log2(math.e), 1.0).astype(F32)
        cos, sa, sb = cos_ref[...] * scale, sa_ref[...] * scale, sb_ref[...] * scale
        half = ROT_DIM // 2

        def rope(acc, cols):
            parts = []
            for c in range(acc.shape[1] // LANES):
                a = acc[:, c * LANES:(c + 1) * LANES]
                parts.append(a * cos + pltpu.roll(a, half, 1) * sa
                             + pltpu.roll(a, LANES - half, 1) * sb)
            return jnp.concatenate(parts, axis=-1)

        project(rope)

    @pl.when(j == 2)
    def _():
        project(lambda acc, cols: acc)

    @pl.when(jnp.logical_or(j == 3, j == 4))
    def _():
        project(lambda acc, cols: _gelu(acc))

    @pl.when(j >= 5)
    def _():
        project(lambda acc, cols: jax.nn.sigmoid(acc + bg_ref[:, cols]))


def _inproj(x, mod3, norm_g, w_in, b_gate, cos_t, sa_t, sb_t):
    n, d = x.shape
    tm = min(TM_PROJ, SEQ)
    tiles_per_seq = SEQ // tm
    return pl.pallas_call(
        _inproj_kernel,
        grid=(n // tm, N_SEG),
        in_specs=[
            pl.BlockSpec((tm, d), lambda i, j: (i, 0)),
            pl.BlockSpec((None, 1, d), lambda i, j: ((i // tiles_per_seq) * N_MOD + 0, 0, 0)),
            pl.BlockSpec((None, 1, d), lambda i, j: ((i // tiles_per_seq) * N_MOD + 1, 0, 0)),
            pl.BlockSpec((1, d), lambda i, j: (0, 0)),
            pl.BlockSpec((d, d), lambda i, j: (0, j)),
            pl.BlockSpec((1, d), lambda i, j: (0, jnp.maximum(j - 5, 0))),
            pl.BlockSpec((tm, LANES), lambda i, j: (i % tiles_per_seq, 0)),
            pl.BlockSpec((tm, LANES), lambda i, j: (i % tiles_per_seq, 0)),
            pl.BlockSpec((tm, LANES), lambda i, j: (i % tiles_per_seq, 0)),
        ],
        out_specs=pl.BlockSpec((tm, d), lambda i, j: (i, j)),
        out_shape=jax.ShapeDtypeStruct((n, N_SEG * d), BF16),
        scratch_shapes=[pltpu.VMEM((tm, d), BF16)],
        compiler_params=_cparams(("arbitrary", "arbitrary")),
        name="in_proj",
    )(x, mod3, mod3, norm_g, w_in, b_gate, cos_t, sa_t, sb_t)


def _attn_kernel(q_ref, k_ref, v_ref, lam_ref, sg_ref, o_ref, vaug_ref, *, lam_init):
    tq, vd = q_ref.shape
    s_len = k_ref.shape[0]
    tk = min(TK, s_len)

    @pl.when(pl.program_id(2) == 0)
    def _():
        vaug_ref[:, :vd] = v_ref[...]
        vaug_ref[:, vd:] = jnp.ones((s_len, vd), BF16)

    q = q_ref[...]
    lane = lax.broadcasted_iota(I32, q.shape, 1)
    zero = jnp.zeros_like(q)
    qq = jnp.concatenate([jnp.where(lane < HEAD_DIM, q, zero),
                          jnp.where(lane >= HEAD_DIM, q, zero)], axis=0)
    dn = (((1,), (1,)), ((), ()))
    m = jnp.full((2 * tq, 1), -jnp.inf, F32)
    acc = jnp.zeros((2 * tq, 2 * vd), F32)
    for c in range(s_len // tk):
        s = lax.dot_general(qq, k_ref[c * tk:(c + 1) * tk, :], dn, preferred_element_type=F32)
        m_new = jnp.maximum(m, jnp.max(s, axis=-1, keepdims=True))
        alpha = jnp.exp2(m - m_new)
        p = jnp.exp2(s - m_new).astype(BF16)
        acc = alpha * acc + jnp.dot(p, vaug_ref[c * tk:(c + 1) * tk, :],
                                    preferred_element_type=F32)
        m = m_new
    on = acc[:, :vd] / acc[:, vd:]
    o0, o1 = on[:tq], on[tq:]
    lp = lam_ref[...]
    lam = (jnp.exp(jnp.sum(lp[0:1] * lp[1:2], axis=-1, keepdims=True))
           - jnp.exp(jnp.sum(lp[2:3] * lp[3:4], axis=-1, keepdims=True)) + lam_init)
    o = o0 - lam * o1
    r = o * lax.rsqrt(jnp.mean(o * o, axis=-1, keepdims=True) + EPS)
    o_ref[...] = ((r * sg_ref[...]) * (1.0 - lam_init)).astype(BF16)


def _attention(proj3, lam, subln_g, lam_init):
    bt, s, _ = proj3.shape
    vd = 2 * HEAD_DIM
    tq = min(TQ, s)
    return pl.pallas_call(
        functools.partial(_attn_kernel, lam_init=lam_init),
        grid=(bt, N_HEADS, s // tq),
        in_specs=[
            pl.BlockSpec((None, tq, vd), lambda b, h, i: (b, i, h)),
            pl.BlockSpec((None, s, vd), lambda b, h, i: (b, 0, N_HEADS + h)),
            pl.BlockSpec((None, s, vd), lambda b, h, i: (b, 0, 2 * N_HEADS + h)),
            pl.BlockSpec((4, HEAD_DIM), lambda b, h, i: (0, 0)),
            pl.BlockSpec((1, vd), lambda b, h, i: (0, 0)),
        ],
        out_specs=pl.BlockSpec((None, tq, vd), lambda b, h, i: (b, i, h)),
        out_shape=jax.ShapeDtypeStruct((bt, s, N_HEADS * vd), BF16),
        scratch_shapes=[pltpu.VMEM((s, 2 * vd), BF16)],
        compiler_params=_cparams(("arbitrary", "arbitrary", "arbitrary")),
        name="diff_attn",
    )(proj3, proj3, proj3, lam, subln_g)


def _mix_kernel(u_ref, vg_ref, ga_ref, gg_ref, at_ref, x_ref, lng_ref, lnb_ref, ws_ref, bs_ref,
                wba_ref, wbg_ref, wo_ref, gt1_ref, n2g_ref, sh2_ref, sc2_ref, wrh_ref, wrl_ref,
                x1_ref, h2_ref, aff_ref):
    tm = u_ref.shape[0]
    vg = vg_ref[...].astype(F32)
    mu = jnp.mean(vg, axis=-1, keepdims=True)
    xc = vg - mu
    var = jnp.mean(xc * xc, axis=-1, keepdims=True)
    vn = (xc * lax.rsqrt(var + EPS) * lng_ref[...] + lnb_ref[...]).astype(BF16)
    bs = bs_ref[...]
    cols = []
    for g in range(GMLP_GROUPS):
        rows = []
        for c in range(tm // CHUNK):
            blk = vn[c * CHUNK:(c + 1) * CHUNK, g * LANES:(g + 1) * LANES]
            rows.append(jnp.dot(ws_ref[g], blk, preferred_element_type=F32) + bs[:, g:g + 1])
        cols.append(jnp.concatenate(rows, axis=0))
    mixed = jnp.concatenate(cols, axis=1)
    gm = (u_ref[...].astype(F32) * mixed).astype(BF16)
    o_g = jnp.dot(gm, wbg_ref[...], preferred_element_type=F32)
    o_a = jnp.dot(at_ref[...], wba_ref[...], preferred_element_type=F32)
    merged = (ga_ref[...].astype(F32) * o_a + gg_ref[...].astype(F32) * o_g).astype(BF16)
    mo = jnp.dot(merged, wo_ref[...], preferred_element_type=F32)
    x1 = x_ref[...] + gt1_ref[...] * mo
    x1_ref[...] = x1
    r = x1 * lax.rsqrt(jnp.mean(x1 * x1, axis=-1, keepdims=True) + EPS) * n2g_ref[...]
    h2 = r * (1.0 + sc2_ref[...]) + sh2_ref[...]
    h2h = h2.astype(BF16)
    d = h2.shape[1]
    h2_ref[:, :d] = h2h
    h2l = (h2 - h2h.astype(F32)).astype(BF16)
    wrh = wrh_ref[...]
    logits = (jnp.dot(h2h, wrh, preferred_element_type=F32)
              + jnp.dot(h2l, wrh, preferred_element_type=F32)
              + jnp.dot(h2h, wrl_ref[...], preferred_element_type=F32))
    lane = lax.broadcasted_iota(I32, logits.shape, 1)
    first = lane < N_EXPERTS
    m = jnp.max(jnp.where(first, logits, -jnp.inf), axis=-1, keepdims=True)
    p = jnp.exp(logits - m)
    aff = p / jnp.sum(jnp.where(first, p, 0.0), axis=-1, keepdims=True)
    aff_ref[...] = aff[:, :N_EXPERTS]
    hi = aff.astype(BF16)
    lo = (aff - hi.astype(F32)).astype(BF16)
    h2_ref[:, d:] = jnp.where(first, hi, jnp.where(lane < 2 * N_EXPERTS, lo, jnp.zeros_like(lo)))


def _mix(proj, attn, x, mod3, lng, lnb, ws, bs_t, wba, wbg, wo, n2g, wrh, wrl):
    n, d = x.shape
    tm = min(TM_MIX, SEQ)
    tps = SEQ // tm
    row = lambda i: (i, 0)
    full2 = lambda i: (0, 0)

    def seg(k):
        return pl.BlockSpec((tm, d), lambda i: (i, k))

    def modspec(k):
        return pl.BlockSpec((None, 1, d), lambda i: ((i // tps) * N_MOD + k, 0, 0))

    return pl.pallas_call(
        _mix_kernel,
        grid=(n // tm,),
        in_specs=[
            seg(3), seg(4), seg(5), seg(6),
            pl.BlockSpec((tm, d), row),
            pl.BlockSpec((tm, d), row),
            pl.BlockSpec((1, d), full2),
            pl.BlockSpec((1, d), full2),
            pl.BlockSpec((GMLP_GROUPS, CHUNK, CHUNK), lambda i: (0, 0, 0)),
            pl.BlockSpec((CHUNK, GMLP_GROUPS), full2),
            pl.BlockSpec((d, d), full2),
            pl.BlockSpec((d, d), full2),
            pl.BlockSpec((d, d), full2),
            modspec(2),
            pl.BlockSpec((1, d), full2),
            modspec(3),
            modspec(4),
            pl.BlockSpec((d, LANES), full2),
            pl.BlockSpec((d, LANES), full2),
        ],
        out_specs=[
            pl.BlockSpec((tm, d), row),
            pl.BlockSpec((tm, d + LANES), row),
            pl.BlockSpec((tm, N_EXPERTS), row),
        ],
        out_shape=[
            jax.ShapeDtypeStruct((n, d), F32),
            jax.ShapeDtypeStruct((n, d + LANES), BF16),
            jax.ShapeDtypeStruct((n, N_EXPERTS), F32),
        ],
        compiler_params=_cparams(("arbitrary",)),
        name="branch_mix",
    )(proj, proj, proj, proj, attn, x, lng, lnb, ws, bs_t, wba, wbg, wo, mod3, n2g, mod3, mod3,
      wrh, wrl)


def _route_kernel(bits_ref, pos_ref, s0_ref, cnt_ref, *, groups):
    ne, tb = bits_ref.shape[1], bits_ref.shape[2]
    ri = lax.broadcasted_iota(I32, (tb, tb), 0)
    ci = lax.broadcasted_iota(I32, (tb, tb), 1)
    tri = jnp.where(ri < ci, 1.0, 0.0).astype(BF16)
    blk_lane = lax.broadcasted_iota(I32, s0_ref.shape, 1)
    s0_ref[...] = jnp.zeros(s0_ref.shape, I32)
    cnt_ref[...] = jnp.zeros(cnt_ref.shape, I32)

    for (b0, nb, cap) in groups:
        def count(pred_fn):
            def body(i, acc):
                return acc + jnp.where(pred_fn(bits_ref[b0 + i]), 1, 0)
            acc = lax.fori_loop(0, nb, body, jnp.zeros((ne, tb), I32))
            return jnp.sum(acc, axis=1, keepdims=True)

        def bs_body(k, thr):
            cand = thr | lax.shift_left(jnp.int32(1), jnp.int32(30) - k)
            c = count(lambda blk: blk >= cand)
            return jnp.where(c >= cap, cand, thr)

        thr = lax.fori_loop(0, 31, bs_body, jnp.zeros((ne, 1), I32))
        n_gt = count(lambda blk: blk > thr)
        need = (cap - n_gt).astype(F32)

        def scan_body(i, carry):
            ceq, cpos = carry
            blk = bits_ref[b0 + i]
            gt = blk > thr
            eq = blk == thr
            eqf = jnp.where(eq, 1.0, 0.0)
            eq_excl = jnp.dot(eqf.astype(BF16), tri, preferred_element_type=F32) + ceq
            self = jnp.where(gt, 1.0, jnp.where(eq_excl < need, eqf, 0.0))
            pos_excl = jnp.dot(self.astype(BF16), tri, preferred_element_type=F32) + cpos
            pos_ref[b0 + i] = jnp.where(self > 0.5, pos_excl.astype(I32), -1)
            n_sel = jnp.sum(self, axis=1, keepdims=True)
            hit = blk_lane == (b0 + i)
            s0_ref[...] = jnp.where(hit, cpos.astype(I32), s0_ref[...])
            cnt_ref[...] = jnp.where(hit, n_sel.astype(I32), cnt_ref[...])
            return (ceq + jnp.sum(eqf, axis=1, keepdims=True), cpos + n_sel)

        zero = jnp.zeros((ne, 1), F32)
        lax.fori_loop(0, nb, scan_body, (zero, zero))


def _route(bits3, groups):
    nblk, ne, tb = bits3.shape
    nbp = -(-nblk // LANES) * LANES
    return pl.pallas_call(
        functools.partial(_route_kernel, groups=groups),
        out_shape=[
            jax.ShapeDtypeStruct((nblk, ne, tb), I32),
            jax.ShapeDtypeStruct((ne, nbp), I32),
            jax.ShapeDtypeStruct((ne, nbp), I32),
        ],
        compiler_params=pltpu.CompilerParams(vmem_limit_bytes=VMEM_LIMIT),
        name="route_topc",
    )(bits3)


def _slot_base(b, e, layout):
    nb0, cp0, cp_tot = layout
    return e * cp_tot + jnp.where(b >= nb0, cp0, 0)


def _one_hot_rows(pos_ref, e, first_slot, tb):
    jrow = lax.broadcasted_iota(I32, (SLAB, tb), 0)
    return jnp.where(jrow == pos_ref[e:e + 1, :] - first_slot, 1.0, 0.0).astype(BF16)


def _one_hot_stack(pos_ref, first_slots, tb):
    return jnp.concatenate([_one_hot_rows(pos_ref, e, first_slots[e], tb)
                            for e in range(N_EXPERTS)], axis=0)


def _window_plan(s0, cnt):
    a0 = (s0 // ROWS_BF16) * ROWS_BF16
    r = s0 - a0
    nch = jnp.where(cnt > 0, (r + cnt + SLAB - 1) // SLAB, 0)
    nxt = ((s0 + cnt) // ROWS_BF16) * ROWS_BF16 - a0
    c1 = jnp.minimum(nxt // SLAB, jnp.maximum(nch - 1, 0))
    o1 = jnp.where(nxt // SLAB == c1, nxt % SLAB, SLAB - ROWS_BF16)
    return a0, r, nch, c1, o1


def _dispatch_kernel(a0_ref, r_ref, nch_ref, c1_ref, o1_ref, h_ref, pos_ref, xe_ref,
                     slab_ref, carry_ref, zero_ref, sems, *, layout, pad_rows):
    b = pl.program_id(0)
    tb, da = h_ref.shape

    @pl.when(b == 0)
    def _():
        carry_ref[...] = jnp.zeros(carry_ref.shape, BF16)

    h = h_ref[...]
    a0s = [a0_ref[b * N_EXPERTS + e] for e in range(N_EXPERTS)]
    nchs = [nch_ref[b * N_EXPERTS + e] for e in range(N_EXPERTS)]
    y0 = jnp.dot(_one_hot_stack(pos_ref, a0s, tb), h, preferred_element_type=F32).astype(BF16)
    row16 = lax.broadcasted_iota(I32, (ROWS_BF16, da), 0)

    def copy(e, c):
        row = pl.multiple_of(_slot_base(b, e, layout) + a0s[e] + c * SLAB, ROWS_BF16)
        return pltpu.make_async_copy(slab_ref.at[e, c], xe_ref.at[pl.ds(row, SLAB)], sems.at[e, c])

    for e in range(N_EXPERTS):
        @pl.when(nchs[e] > 0)
        def _(e=e):
            ye = y0[e * SLAB:(e + 1) * SLAB]
            slab_ref[e, 0, :ROWS_BF16, :] = jnp.where(row16 < r_ref[b * N_EXPERTS + e],
                                                      carry_ref[e], ye[:ROWS_BF16])
            slab_ref[e, 0, ROWS_BF16:, :] = ye[ROWS_BF16:]
            copy(e, 0).start()

            def extra(c, carry):
                pt = _one_hot_rows(pos_ref, e, a0s[e] + c * SLAB, tb)
                slab_ref[e, c] = jnp.dot(pt, h, preferred_element_type=F32).astype(BF16)
                copy(e, c).start()
                return carry

            lax.fori_loop(1, nchs[e], extra, 0)
            o1 = pl.multiple_of(o1_ref[b * N_EXPERTS + e], ROWS_BF16)
            carry_ref[e] = slab_ref[e, c1_ref[b * N_EXPERTS + e], pl.ds(o1, ROWS_BF16), :]

    for e in range(N_EXPERTS):
        def drain(c, carry, e=e):
            copy(e, c).wait()
            return carry

        lax.fori_loop(0, nchs[e], drain, 0)

    for (last_b, first_pad) in pad_rows:
        @pl.when(b == last_b)
        def _(last_b=last_b, first_pad=first_pad):
            zero_ref[...] = jnp.zeros(zero_ref.shape, BF16)
            cps = []
            for e in range(N_EXPERTS):
                for c in range(PAD // SLAB):
                    r0 = _slot_base(last_b, e, layout) + first_pad + c * SLAB
                    cp = pltpu.make_async_copy(zero_ref, xe_ref.at[pl.ds(r0, SLAB)], sems.at[e, c])
                    cp.start()
                    cps.append(cp)
            for cp in cps:
                cp.wait()


def _dispatch(plan, h2a, pos3, layout, pad_rows, rows):
    n, da = h2a.shape
    grid_spec = pltpu.PrefetchScalarGridSpec(
        num_scalar_prefetch=5,
        grid=(n // TB,),
        in_specs=[
            pl.BlockSpec((TB, da), lambda b, *_: (b, 0)),
            pl.BlockSpec((None, N_EXPERTS, TB), lambda b, *_: (b, 0, 0)),
        ],
        out_specs=pl.BlockSpec(memory_space=pl.ANY),
        scratch_shapes=[
            pltpu.VMEM((N_EXPERTS, MAXCH, SLAB, da), BF16),
            pltpu.VMEM((N_EXPERTS, ROWS_BF16, da), BF16),
            pltpu.VMEM((SLAB, da), BF16),
            pltpu.SemaphoreType.DMA((N_EXPERTS, MAXCH)),
        ],
    )
    return pl.pallas_call(
        functools.partial(_dispatch_kernel, layout=layout, pad_rows=pad_rows),
        grid_spec=grid_spec,
        out_shape=jax.ShapeDtypeStruct((rows, da), BF16),
        compiler_params=_cparams(("arbitrary",)),
        name="moe_dispatch",
    )(*plan, h2a, pos3)


def _ffn_kernel(x_ref, wg_ref, wu_ref, wd_ref, y_ref):
    e = pl.program_id(0)
    d = wg_ref.shape[0]
    x = x_ref[:, :d]
    gl = x_ref[:, d:].astype(F32)
    lane = lax.broadcasted_iota(I32, gl.shape, 1)
    mine = jnp.logical_or(lane == e, lane == e + N_EXPERTS)
    gate = jnp.sum(jnp.where(mine, gl, 0.0), axis=-1, keepdims=True)
    g = jnp.dot(x, wg_ref[...], preferred_element_type=F32)
    u = jnp.dot(x, wu_ref[...], preferred_element_type=F32)
    hid = (g * jax.nn.sigmoid(g) * u).astype(BF16)
    y_ref[...] = (jnp.dot(hid, wd_ref[...], preferred_element_type=F32) * gate).astype(BF16)


def _ffn_tile(cp_tot):
    for tm in (512, 256, 128, 64):
        if cp_tot % tm == 0:
            return tm
    raise ValueError("slot rows per expert must be a multiple of 64")


def _ffn(xe, wg, wu, wd, cp_tot):
    rows, da = xe.shape
    d, ff = wg.shape[1], wg.shape[2]
    tm = _ffn_tile(cp_tot)
    tiles = cp_tot // tm
    return pl.pallas_call(
        _ffn_kernel,
        grid=(N_EXPERTS, tiles),
        in_specs=[
            pl.BlockSpec((tm, da), lambda e, r: (e * tiles + r, 0)),
            pl.BlockSpec((None, d, ff), lambda e, r: (e, 0, 0), pipeline_mode=pl.Buffered(1)),
            pl.BlockSpec((None, d, ff), lambda e, r: (e, 0, 0), pipeline_mode=pl.Buffered(1)),
            pl.BlockSpec((None, ff, d), lambda e, r: (e, 0, 0), pipeline_mode=pl.Buffered(1)),
        ],
        out_specs=pl.BlockSpec((tm, d), lambda e, r: (e * tiles + r, 0)),
        out_shape=jax.ShapeDtypeStruct((rows, d), BF16),
        compiler_params=_cparams(("arbitrary", "arbitrary")),
        name="expert_ffn",
    )(xe, wg, wu, wd)


def _combine_kernel(a0_ref, nch_ref, ye_ref, pos_ref, x_ref, gt_ref, fg_ref, o_ref,
                    win_ref, xwin_ref, acc_ref, sems, *, layout, final):
    b = pl.program_id(0)
    tb, d = x_ref.shape
    dn = (((0,), (0,)), ((), ()))
    a0s = [a0_ref[b * N_EXPERTS + e] for e in range(N_EXPERTS)]
    nchs = [nch_ref[b * N_EXPERTS + e] for e in range(N_EXPERTS)]

    def copy(e, c):
        row = pl.multiple_of(_slot_base(b, e, layout) + a0s[e] + c * SLAB, ROWS_BF16)
        dst = win_ref.at[pl.ds(e * SLAB, SLAB)] if isinstance(c, int) else xwin_ref.at[e, c - 1]
        return pltpu.make_async_copy(ye_ref.at[pl.ds(row, SLAB)], dst, sems.at[e, c])

    for e in range(N_EXPERTS):
        copy(e, 0).start()
    for e in range(N_EXPERTS):
        def fetch(c, carry, e=e):
            copy(e, c).start()
            return carry

        lax.fori_loop(1, nchs[e], fetch, 0)

    pt = _one_hot_stack(pos_ref, a0s, tb)
    for e in range(N_EXPERTS):
        copy(e, 0).wait()
    acc_ref[...] = lax.dot_general(pt, win_ref[...], dn, preferred_element_type=F32)

    for e in range(N_EXPERTS):
        def extra(c, carry, e=e):
            copy(e, c).wait()
            ptc = _one_hot_rows(pos_ref, e, a0s[e] + c * SLAB, tb)
            acc_ref[...] += lax.dot_general(ptc, xwin_ref[e, c - 1], dn, preferred_element_type=F32)
            return carry

        lax.fori_loop(1, nchs[e], extra, 0)

    x2 = x_ref[...] + gt_ref[...] * acc_ref[...]
    if final:
        x2 = x2 * lax.rsqrt(jnp.mean(x2 * x2, axis=-1, keepdims=True) + EPS) * fg_ref[...]
    o_ref[...] = x2


def _combine(a0, nch, ye, pos3, x1, mod3, final_g, layout, final):
    n, d = x1.shape
    tps = SEQ // TB
    grid_spec = pltpu.PrefetchScalarGridSpec(
        num_scalar_prefetch=2,
        grid=(n // TB,),
        in_specs=[
            pl.BlockSpec(memory_space=pl.ANY),
            pl.BlockSpec((None, N_EXPERTS, TB), lambda b, *_: (b, 0, 0)),
            pl.BlockSpec((TB, d), lambda b, *_: (b, 0)),
            pl.BlockSpec((None, 1, d), lambda b, *_: ((b // tps) * N_MOD + 5, 0, 0)),
            pl.BlockSpec((1, d), lambda b, *_: (0, 0)),
        ],
        out_specs=pl.BlockSpec((TB, d), lambda b, *_: (b, 0)),
        scratch_shapes=[
            pltpu.VMEM((N_EXPERTS * SLAB, d), BF16),
            pltpu.VMEM((N_EXPERTS, MAXCH - 1, SLAB, d), BF16),
            pltpu.VMEM((TB, d), F32),
            pltpu.SemaphoreType.DMA((N_EXPERTS, MAXCH)),
        ],
    )
    return pl.pallas_call(
        functools.partial(_combine_kernel, layout=layout, final=final),
        grid_spec=grid_spec,
        out_shape=jax.ShapeDtypeStruct((n, d), F32),
        compiler_params=_cparams(("arbitrary",)),
        name="moe_combine",
    )(a0, nch, ye, pos3, x1, mod3, final_g)


def _rope_tables():
    half = ROT_DIM // 2
    inv = ROPE_THETA ** (-jnp.arange(0, ROT_DIM, 2, dtype=F32) / ROT_DIM)
    ang = jnp.arange(SEQ, dtype=F32)[:, None] * inv[None, :]
    cos, sin = jnp.cos(ang), jnp.sin(ang)
    dd = jnp.arange(LANES) % HEAD_DIM
    cos_l = jnp.where(dd[None, :] < ROT_DIM, cos[:, dd % half], 1.0)
    sin_l = sin[:, dd % half]
    sa = jnp.where((dd[None, :] >= half) & (dd[None, :] < ROT_DIM), sin_l, 0.0)
    sb = jnp.where(dd[None, :] < half, -sin_l, 0.0)
    return cos_l.astype(F32), sa.astype(F32), sb.astype(F32)


def _trunk(xs, cs, norm1_g, w_ada, b_ada, w_in, b_gate, lam, subln_g, gmlp_ln_g, gmlp_ln_b,
           w_spatial, b_spatial, w_br_attn, w_br_gmlp, w_out, norm2_g, w_router,
           w_e_gate, w_e_up, w_e_down, final_g):
    d = D_MODEL
    assert N_HEADS * 2 * HEAD_DIM == d and GMLP_GROUPS * LANES == d and CHUNK == LANES
    assert SEQ % TB == 0 and TB % CHUNK == 0
    batches = [x.shape[0] for x in xs]
    bt = sum(batches)
    n = bt * SEQ
    x = jnp.concatenate([xx.reshape(-1, d) for xx in xs], axis=0)
    c = jnp.concatenate(cs, axis=0)
    bp = -(-bt // 8) * 8
    c_pad = jnp.pad(c, ((0, bp - bt), (0, 0)))
    mod = _ada(c_pad, w_ada, b_ada)

    caps = [CAPACITY_FACTOR * b * SEQ // N_EXPERTS for b in batches]
    nbs = [b * SEQ // TB for b in batches]
    cps = [cap + PAD for cap in caps]
    cp_tot = sum(cps)
    layout = (nbs[0], cps[0], cp_tot)
    groups = ((0, nbs[0], caps[0]), (nbs[0], nbs[1], caps[1]))
    pad_rows = ((nbs[0] - 1, caps[0]), (nbs[0] + nbs[1] - 1, caps[1]))
    rows = N_EXPERTS * cp_tot
    nblk = n // TB

    cos_t, sa_t, sb_t = _rope_tables()
    y = None
    for l in range(DEPTH):
        lam_init = 0.8 - 0.6 * math.exp(-0.3 * l)
        mod3 = mod[l, :bt].reshape(bt * N_MOD, 1, d)
        proj = _inproj(x, mod3, norm1_g[l][None], w_in[l].astype(BF16), b_gate[l][None],
                       cos_t, sa_t, sb_t)
        attn = _attention(proj.reshape(bt, SEQ, N_SEG * d), lam[l], subln_g[l][None], lam_init)
        wr = jnp.pad(jnp.concatenate([w_router[l], w_router[l]], axis=1),
                     ((0, 0), (0, LANES - 2 * N_EXPERTS)))
        wrh = wr.astype(BF16)
        wrl = (wr - wrh.astype(F32)).astype(BF16)
        x1, h2a, aff = _mix(proj, attn.reshape(n, d), x, mod3, gmlp_ln_g[l][None],
                            gmlp_ln_b[l][None], w_spatial[l].astype(BF16), b_spatial[l].T,
                            w_br_attn[l].astype(BF16), w_br_gmlp[l].astype(BF16),
                            w_out[l].astype(BF16), norm2_g[l][None], wrh, wrl)
        bits3 = lax.bitcast_convert_type(aff, I32).reshape(nblk, TB, N_EXPERTS).transpose(0, 2, 1)
        pos3, s0_t, cnt_t = _route(bits3, groups)
        plan = _window_plan(s0_t[:, :nblk].T.reshape(-1), cnt_t[:, :nblk].T.reshape(-1))
        xe = _dispatch(plan, h2a, pos3, layout, pad_rows, rows)
        ye = _ffn(xe, w_e_gate[l].astype(BF16), w_e_up[l].astype(BF16),
                  w_e_down[l].astype(BF16), cp_tot)
        final = l == DEPTH - 1
        x = _combine(plan[0], plan[2], ye, pos3, x1, mod3, final_g[None], layout, final)
    outs = []
    off = 0
    for b in batches:
        outs.append(x[off:off + b * SEQ].reshape(b, SEQ, d))
        off += b * SEQ
    return tuple(outs)


def kernel(x_prompt, x_sample, c_prompt, c_sample, norm1_g, w_ada, b_ada, w_in, b_gate, lam, subln_g, gmlp_ln_g, gmlp_ln_b, w_spatial, b_spatial, w_br_attn, w_br_gmlp, w_out, norm2_g, w_router, w_e_gate, w_e_up, w_e_down, final_g):
    return _trunk((x_prompt, x_sample), (c_prompt, c_sample), norm1_g, w_ada, b_ada, w_in,
                  b_gate, lam, subln_g, gmlp_ln_g, gmlp_ln_b, w_spatial, b_spatial, w_br_attn,
                  w_br_gmlp, w_out, norm2_g, w_router, w_e_gate, w_e_up, w_e_down, final_g)
```

```python
import functools
import math

import jax
import jax.numpy as jnp
from jax import lax
from jax.experimental import pallas as pl
from jax.experimental.pallas import tpu as pltpu

F32 = jnp.float32
BF16 = jnp.bfloat16
I32 = jnp.int32

D_MODEL = 1024
SEQ = 4096
DEPTH = 2
N_HEADS = 8
HEAD_DIM = 64
ROT_DIM = HEAD_DIM // 4
ROPE_THETA = 500000.0
CHUNK = 128
GMLP_GROUPS = 8
N_EXPERTS = 16
EXPERT_FF = 2048
CAPACITY_FACTOR = 2
N_MOD = 6
N_SEG = 7
EPS = 1e-6

LANES = 128
VMEM_LIMIT = 56 * 1024 * 1024

TM_PROJ = 512
NC_PROJ = 256
TQ = 256
TK = 1024
SCORE_BOUND = 60.0
TM_MIX = 256
TB = 256
SLAB = 64
ROWS_BF16 = 16
MAXCH = -(-(TB + ROWS_BF16 - 1) // SLAB)
PAD = TB


def _cparams(sem):
    return pltpu.CompilerParams(dimension_semantics=sem, vmem_limit_bytes=VMEM_LIMIT)


def _ada_kernel(c_ref, w_ref, b_ref, o_ref):
    c = c_ref[...]
    a = c * jax.nn.sigmoid(c)
    o_ref[0] = jnp.dot(a, w_ref[0], preferred_element_type=F32,
                       precision=lax.Precision.HIGHEST) + b_ref[0]


def _ada(c_pad, w_ada, b_ada):
    bp, d = c_pad.shape
    depth = w_ada.shape[0]
    return pl.pallas_call(
        _ada_kernel,
        grid=(depth, N_MOD),
        in_specs=[
            pl.BlockSpec((bp, d), lambda l, j: (0, 0)),
            pl.BlockSpec((1, d, d), lambda l, j: (l, 0, j)),
            pl.BlockSpec((1, 1, d), lambda l, j: (l, 0, j)),
        ],
        out_specs=pl.BlockSpec((1, bp, d), lambda l, j: (l, 0, j)),
        out_shape=jax.ShapeDtypeStruct((depth, bp, N_MOD * d), F32),
        compiler_params=_cparams(("arbitrary", "arbitrary")),
        name="ada_mod",
    )(c_pad, w_ada, b_ada.reshape(depth, 1, N_MOD * d))


def _gelu(x):
    return jax.nn.gelu(x)


def _cast_kernel(x_ref, o_ref):
    o_ref[...] = x_ref[...].astype(o_ref.dtype)


def _to_bf16(w, row_block=None):
    lead, r, c = w.shape
    rb = r if row_block is None else row_block
    spec = pl.BlockSpec((None, rb, c), lambda i, k: (i, k, 0))
    return pl.pallas_call(
        _cast_kernel,
        grid=(lead, r // rb),
        in_specs=[spec],
        out_specs=spec,
        out_shape=jax.ShapeDtypeStruct(w.shape, BF16),
        compiler_params=_cparams(("arbitrary", "arbitrary")),
        name="to_bf16",
    )(w)


def _inproj_kernel(xa_ref, xb_ref, sh_ref, sc_ref, g_ref, w_ref, bg_ref, cos_ref, sa_ref, sb_ref,
                   o_ref, h_ref, *, tiles_a):
    j = pl.program_id(1)

    @pl.when(j == 0)
    def _():
        x = jnp.where(pl.program_id(0) < tiles_a, xa_ref[...], xb_ref[...])
        r = x * lax.rsqrt(jnp.mean(x * x, axis=-1, keepdims=True) + EPS) * g_ref[...]
        h_ref[...] = (r * (1.0 + sc_ref[...]) + sh_ref[...]).astype(BF16)

    def project(epilogue):
        h = h_ref[...]
        for c in range(w_ref.shape[1] // NC_PROJ):
            cols = slice(c * NC_PROJ, (c + 1) * NC_PROJ)
            acc = jnp.dot(h, w_ref[:, cols], preferred_element_type=F32)
            o_ref[:, cols] = epilogue(acc, cols).astype(BF16)

    @pl.when(j < 2)
    def _():
        scale = jnp.where(j == 0, HEAD_DIM ** -0.5 * math.log2(math.e), 1.0).astype(F32)
        cos, sa, sb = cos_ref[...] * scale, sa_ref[...] * scale, sb_ref[...] * scale
        half = ROT_DIM // 2

        def rope(acc, cols):
            parts = []
            for c in range(acc.shape[1] // LANES):
                a = acc[:, c * LANES:(c + 1) * LANES]
                parts.append(a * cos + pltpu.roll(a, half, 1) * sa
                             + pltpu.roll(a, LANES - half, 1) * sb)
            return jnp.concatenate(parts, axis=-1)

        project(rope)

    @pl.when(j == 2)
    def _():
        project(lambda acc, cols: acc)

    @pl.when(jnp.logical_or(j == 3, j == 4))
    def _():
        project(lambda acc, cols: _gelu(acc))

    @pl.when(j >= 5)
    def _():
        project(lambda acc, cols: jax.nn.sigmoid(acc + bg_ref[:, cols]))


def _two_group_specs(rows, d, tm, rows_a, xb_row0):
    ta, tb0 = rows_a // tm, xb_row0 // tm
    return ta, [pl.BlockSpec((tm, d), lambda i, *_: (jnp.minimum(i, ta - 1), 0)),
                pl.BlockSpec((tm, d), lambda i, *_: (jnp.maximum(i - ta, 0) + tb0, 0))]


def _inproj(xa, xb, n, rows_a, xb_row0, mod3, norm_g, w_in, layer, b_gate, cos_t, sa_t, sb_t):
    d = xa.shape[1]
    tm = min(TM_PROJ, SEQ)
    tiles_per_seq = SEQ // tm
    tiles_a, xspecs = _two_group_specs(n, d, tm, rows_a, xb_row0)
    return pl.pallas_call(
        functools.partial(_inproj_kernel, tiles_a=tiles_a),
        grid=(n // tm, N_SEG),
        in_specs=xspecs + [
            pl.BlockSpec((None, 1, d), lambda i, j: ((i // tiles_per_seq) * N_MOD + 0, 0, 0)),
            pl.BlockSpec((None, 1, d), lambda i, j: ((i // tiles_per_seq) * N_MOD + 1, 0, 0)),
            pl.BlockSpec((1, d), lambda i, j: (0, 0)),
            pl.BlockSpec((None, d, d), lambda i, j: (layer, 0, j)),
            pl.BlockSpec((1, d), lambda i, j: (0, jnp.maximum(j - 5, 0))),
            pl.BlockSpec((tm, LANES), lambda i, j: (i % tiles_per_seq, 0)),
            pl.BlockSpec((tm, LANES), lambda i, j: (i % tiles_per_seq, 0)),
            pl.BlockSpec((tm, LANES), lambda i, j: (i % tiles_per_seq, 0)),
        ],
        out_specs=pl.BlockSpec((tm, d), lambda i, j: (i, j)),
        out_shape=jax.ShapeDtypeStruct((n, N_SEG * d), BF16),
        scratch_shapes=[pltpu.VMEM((tm, d), BF16)],
        compiler_params=_cparams(("arbitrary", "arbitrary")),
        name="in_proj",
    )(xa, xb, mod3, mod3, norm_g, w_in, b_gate, cos_t, sa_t, sb_t)


def _attn_kernel(q_ref, k_ref, v_ref, lam_ref, sg_ref, o_ref, vaug_ref, p_ref, kn_ref, *,
                 lam_init):
    tq, vd = q_ref.shape
    s_len = k_ref.shape[0]
    tk = min(TK, s_len)

    @pl.when(pl.program_id(2) == 0)
    def _():
        vaug_ref[:, :vd] = v_ref[...]
        vaug_ref[:, vd:] = jnp.ones((s_len, vd), BF16)
        kf = k_ref[...].astype(F32)
        kn_ref[0] = jnp.max(jnp.sum(kf * kf, axis=-1, keepdims=True))

    q = q_ref[...]
    qf = q.astype(F32)
    qn = jnp.max(jnp.sum(qf * qf, axis=-1, keepdims=True))
    lane = lax.broadcasted_iota(I32, q.shape, 1)
    zero = jnp.zeros_like(q)
    qq = jnp.concatenate([jnp.where(lane < HEAD_DIM, q, zero),
                          jnp.where(lane >= HEAD_DIM, q, zero)], axis=0)
    dn = (((1,), (1,)), ((), ()))
    lp = lam_ref[...]
    lam = (jnp.exp(jnp.sum(lp[0:1] * lp[1:2], axis=-1, keepdims=True))
           - jnp.exp(jnp.sum(lp[2:3] * lp[3:4], axis=-1, keepdims=True)) + lam_init)

    def finish(o):
        r = o * lax.rsqrt(jnp.mean(o * o, axis=-1, keepdims=True) + EPS)
        o_ref[...] = ((r * sg_ref[...]) * (1.0 - lam_init)).astype(BF16)

    small = qn * kn_ref[0] <= SCORE_BOUND * SCORE_BOUND

    @pl.when(small)
    def _():
        l = jnp.zeros((2 * tq, 1), F32)
        for c in range(s_len // tk):
            s = lax.dot_general(qq, k_ref[c * tk:(c + 1) * tk, :], dn, preferred_element_type=F32)
            p = jnp.exp2(s)
            l = l + jnp.sum(p, axis=-1, keepdims=True)
            p_ref[:, c * tk:(c + 1) * tk] = p
        a0 = 1.0 / l[:tq]
        a1 = lam / l[tq:]
        acc = jnp.zeros((tq, vd), F32)
        for c in range(s_len // tk):
            pd = p_ref[:tq, c * tk:(c + 1) * tk] * a0 - p_ref[tq:, c * tk:(c + 1) * tk] * a1
            acc = acc + jnp.dot(pd.astype(BF16), v_ref[c * tk:(c + 1) * tk, :],
                                preferred_element_type=F32)
        finish(acc)

    @pl.when(jnp.logical_not(small))
    def _():
        m = jnp.full((2 * tq, 1), -jnp.inf, F32)
        acc = jnp.zeros((2 * tq, 2 * vd), F32)
        for c in range(s_len // tk):
            s = lax.dot_general(qq, k_ref[c * tk:(c + 1) * tk, :], dn, preferred_element_type=F32)
            m_new = jnp.maximum(m, jnp.max(s, axis=-1, keepdims=True))
            alpha = jnp.exp2(m - m_new)
            p = jnp.exp2(s - m_new).astype(BF16)
            acc = alpha * acc + jnp.dot(p, vaug_ref[c * tk:(c + 1) * tk, :],
                                        preferred_element_type=F32)
            m = m_new
        on = acc[:, :vd] / acc[:, vd:]
        finish(on[:tq] - lam * on[tq:])


def _attention(proj3, lam, subln_g, lam_init):
    bt, s, _ = proj3.shape
    vd = 2 * HEAD_DIM
    tq = min(TQ, s)
    return pl.pallas_call(
        functools.partial(_attn_kernel, lam_init=lam_init),
        grid=(bt, N_HEADS, s // tq),
        in_specs=[
            pl.BlockSpec((None, tq, vd), lambda b, h, i: (b, i, h)),
            pl.BlockSpec((None, s, vd), lambda b, h, i: (b, 0, N_HEADS + h)),
            pl.BlockSpec((None, s, vd), lambda b, h, i: (b, 0, 2 * N_HEADS + h)),
            pl.BlockSpec((4, HEAD_DIM), lambda b, h, i: (0, 0)),
            pl.BlockSpec((1, vd), lambda b, h, i: (0, 0)),
        ],
        out_specs=pl.BlockSpec((None, tq, vd), lambda b, h, i: (b, i, h)),
        out_shape=jax.ShapeDtypeStruct((bt, s, N_HEADS * vd), BF16),
        scratch_shapes=[pltpu.VMEM((s, 2 * vd), BF16), pltpu.VMEM((2 * tq, s), F32),
                        pltpu.SMEM((1,), F32)],
        compiler_params=_cparams(("arbitrary", "arbitrary", "arbitrary")),
        name="diff_attn",
    )(proj3, proj3, proj3, lam, subln_g)


def _mix_kernel(u_ref, vg_ref, ga_ref, gg_ref, at_ref, xa_ref, xb_ref, lng_ref, lnb_ref, ws_ref,
                bs_ref, wba_ref, wbg_ref, wo_ref, gt1_ref, n2g_ref, sh2_ref, sc2_ref, wrh_ref,
                wrl_ref, x1_ref, h2_ref, aff_ref, *, tiles_a):
    tm = u_ref.shape[0]
    vg = vg_ref[...].astype(F32)
    mu = jnp.mean(vg, axis=-1, keepdims=True)
    xc = vg - mu
    var = jnp.mean(xc * xc, axis=-1, keepdims=True)
    vn = (xc * lax.rsqrt(var + EPS) * lng_ref[...] + lnb_ref[...]).astype(BF16)
    bs = bs_ref[...]
    cols = []
    for g in range(GMLP_GROUPS):
        rows = []
        for c in range(tm // CHUNK):
            blk = vn[c * CHUNK:(c + 1) * CHUNK, g * LANES:(g + 1) * LANES]
            rows.append(jnp.dot(ws_ref[g], blk, preferred_element_type=F32) + bs[:, g:g + 1])
        cols.append(jnp.concatenate(rows, axis=0))
    mixed = jnp.concatenate(cols, axis=1)
    gm = (u_ref[...].astype(F32) * mixed).astype(BF16)
    o_g = jnp.dot(gm, wbg_ref[...], preferred_element_type=F32)
    o_a = jnp.dot(at_ref[...], wba_ref[...], preferred_element_type=F32)
    merged = (ga_ref[...].astype(F32) * o_a + gg_ref[...].astype(F32) * o_g).astype(BF16)
    mo = jnp.dot(merged, wo_ref[...], preferred_element_type=F32)
    x_in = jnp.where(pl.program_id(0) < tiles_a, xa_ref[...], xb_ref[...])
    x1 = x_in + gt1_ref[...] * mo
    x1_ref[...] = x1
    r = x1 * lax.rsqrt(jnp.mean(x1 * x1, axis=-1, keepdims=True) + EPS) * n2g_ref[...]
    h2 = r * (1.0 + sc2_ref[...]) + sh2_ref[...]
    h2h = h2.astype(BF16)
    d = h2.shape[1]
    h2_ref[:, :d] = h2h
    h2l = (h2 - h2h.astype(F32)).astype(BF16)
    wrh = wrh_ref[...]
    logits = (jnp.dot(h2h, wrh, preferred_element_type=F32)
              + jnp.dot(h2l, wrh, preferred_element_type=F32)
              + jnp.dot(h2h, wrl_ref[...], preferred_element_type=F32))
    lane = lax.broadcasted_iota(I32, logits.shape, 1)
    first = lane < N_EXPERTS
    m = jnp.max(jnp.where(first, logits, -jnp.inf), axis=-1, keepdims=True)
    p = jnp.exp(logits - m)
    aff = p / jnp.sum(jnp.where(first, p, 0.0), axis=-1, keepdims=True)
    aff_ref[...] = aff[:, :N_EXPERTS]
    hi = aff.astype(BF16)
    lo = (aff - hi.astype(F32)).astype(BF16)
    h2_ref[:, d:] = jnp.where(first, hi, jnp.where(lane < 2 * N_EXPERTS, lo, jnp.zeros_like(lo)))


def _mix(proj, attn, xa, xb, rows_a, xb_row0, mod3, lng, lnb, ws, bs_t, wba, wbg, wo, n2g, wrh,
         wrl):
    n, d = attn.shape
    tm = min(TM_MIX, SEQ)
    tps = SEQ // tm
    tiles_a, xspecs = _two_group_specs(n, d, tm, rows_a, xb_row0)
    row = lambda i: (i, 0)
    full2 = lambda i: (0, 0)

    def seg(k):
        return pl.BlockSpec((tm, d), lambda i: (i, k))

    def modspec(k):
        return pl.BlockSpec((None, 1, d), lambda i: ((i // tps) * N_MOD + k, 0, 0))

    return pl.pallas_call(
        functools.partial(_mix_kernel, tiles_a=tiles_a),
        grid=(n // tm,),
        in_specs=[
            seg(3), seg(4), seg(5), seg(6),
            pl.BlockSpec((tm, d), row),
        ] + xspecs + [
            pl.BlockSpec((1, d), full2),
            pl.BlockSpec((1, d), full2),
            pl.BlockSpec((GMLP_GROUPS, CHUNK, CHUNK), lambda i: (0, 0, 0)),
            pl.BlockSpec((CHUNK, GMLP_GROUPS), full2),
            pl.BlockSpec((d, d), full2),
            pl.BlockSpec((d, d), full2),
            pl.BlockSpec((d, d), full2),
            modspec(2),
            pl.BlockSpec((1, d), full2),
            modspec(3),
            modspec(4),
            pl.BlockSpec((d, LANES), full2),
            pl.BlockSpec((d, LANES), full2),
        ],
        out_specs=[
            pl.BlockSpec((tm, d), row),
            pl.BlockSpec((tm, d + LANES), row),
            pl.BlockSpec((tm, N_EXPERTS), row),
        ],
        out_shape=[
            jax.ShapeDtypeStruct((n, d), F32),
            jax.ShapeDtypeStruct((n, d + LANES), BF16),
            jax.ShapeDtypeStruct((n, N_EXPERTS), F32),
        ],
        compiler_params=_cparams(("arbitrary",)),
        name="branch_mix",
    )(proj, proj, proj, proj, attn, xa, xb, lng, lnb, ws, bs_t, wba, wbg, wo, mod3, n2g, mod3,
      mod3, wrh, wrl)


def _route_kernel(bits_ref, pos_ref, s0_ref, cnt_ref, *, groups):
    ne, tb = bits_ref.shape[1], bits_ref.shape[2]
    ri = lax.broadcasted_iota(I32, (tb, tb), 0)
    ci = lax.broadcasted_iota(I32, (tb, tb), 1)
    tri = jnp.where(ri < ci, 1.0, 0.0).astype(BF16)
    blk_lane = lax.broadcasted_iota(I32, s0_ref.shape, 1)
    s0_ref[...] = jnp.zeros(s0_ref.shape, I32)
    cnt_ref[...] = jnp.zeros(cnt_ref.shape, I32)

    for (b0, nb, cap) in groups:
        def count(pred_fn):
            def body(i, acc):
                return acc + jnp.where(pred_fn(bits_ref[b0 + i]), 1, 0)
            acc = lax.fori_loop(0, nb, body, jnp.zeros((ne, tb), I32))
            return jnp.sum(acc, axis=1, keepdims=True)

        def bs_body(k, thr):
            cand = thr | lax.shift_left(jnp.int32(1), jnp.int32(30) - k)
            c = count(lambda blk: blk >= cand)
            return jnp.where(c >= cap, cand, thr)

        thr = lax.fori_loop(0, 31, bs_body, jnp.zeros((ne, 1), I32))
        n_gt = count(lambda blk: blk > thr)
        need = (cap - n_gt).astype(F32)

        def scan_body(i, carry):
            ceq, cpos = carry
            blk = bits_ref[b0 + i]
            gt = blk > thr
            eq = blk == thr
            eqf = jnp.where(eq, 1.0, 0.0)
            eq_excl = jnp.dot(eqf.astype(BF16), tri, preferred_element_type=F32) + ceq
            self = jnp.where(gt, 1.0, jnp.where(eq_excl < need, eqf, 0.0))
            pos_excl = jnp.dot(self.astype(BF16), tri, preferred_element_type=F32) + cpos
            pos_ref[b0 + i] = jnp.where(self > 0.5, pos_excl.astype(I32), -1)
            n_sel = jnp.sum(self, axis=1, keepdims=True)
            hit = blk_lane == (b0 + i)
            s0_ref[...] = jnp.where(hit, cpos.astype(I32), s0_ref[...])
            cnt_ref[...] = jnp.where(hit, n_sel.astype(I32), cnt_ref[...])
            return (ceq + jnp.sum(eqf, axis=1, keepdims=True), cpos + n_sel)

        zero = jnp.zeros((ne, 1), F32)
        lax.fori_loop(0, nb, scan_body, (zero, zero))


def _route(bits3, groups):
    nblk, ne, tb = bits3.shape
    nbp = -(-nblk // LANES) * LANES
    return pl.pallas_call(
        functools.partial(_route_kernel, groups=groups),
        out_shape=[
            jax.ShapeDtypeStruct((nblk, ne, tb), I32),
            jax.ShapeDtypeStruct((ne, nbp), I32),
            jax.ShapeDtypeStruct((ne, nbp), I32),
        ],
        compiler_params=pltpu.CompilerParams(vmem_limit_bytes=VMEM_LIMIT),
        name="route_topc",
    )(bits3)


def _slot_base(b, e, layout):
    nb0, cp0, cp_tot = layout
    return e * cp_tot + jnp.where(b >= nb0, cp0, 0)


def _one_hot_rows(pos_ref, e, first_slot, tb):
    jrow = lax.broadcasted_iota(I32, (SLAB, tb), 0)
    return jnp.where(jrow == pos_ref[e:e + 1, :] - first_slot, 1.0, 0.0).astype(BF16)


def _one_hot_stack(pos_ref, first_slots, tb):
    return jnp.concatenate([_one_hot_rows(pos_ref, e, first_slots[e], tb)
                            for e in range(N_EXPERTS)], axis=0)


def _window_plan(s0, cnt):
    a0 = (s0 // ROWS_BF16) * ROWS_BF16
    r = s0 - a0
    nch = jnp.where(cnt > 0, (r + cnt + SLAB - 1) // SLAB, 0)
    nxt = ((s0 + cnt) // ROWS_BF16) * ROWS_BF16 - a0
    c1 = jnp.minimum(nxt // SLAB, jnp.maximum(nch - 1, 0))
    o1 = jnp.where(nxt // SLAB == c1, nxt % SLAB, SLAB - ROWS_BF16)
    return a0, r, nch, c1, o1


def _dispatch_kernel(a0_ref, r_ref, nch_ref, c1_ref, o1_ref, h_ref, pos_ref, xe_ref,
                     slab_ref, carry_ref, zero_ref, sems, *, layout, pad_rows):
    b = pl.program_id(0)
    tb, da = h_ref.shape

    @pl.when(b == 0)
    def _():
        carry_ref[...] = jnp.zeros(carry_ref.shape, BF16)

    h = h_ref[...]
    a0s = [a0_ref[b * N_EXPERTS + e] for e in range(N_EXPERTS)]
    nchs = [nch_ref[b * N_EXPERTS + e] for e in range(N_EXPERTS)]
    y0 = jnp.dot(_one_hot_stack(pos_ref, a0s, tb), h, preferred_element_type=F32).astype(BF16)
    row16 = lax.broadcasted_iota(I32, (ROWS_BF16, da), 0)

    def copy(e, c, blk=b):
        row = pl.multiple_of(_slot_base(blk, e, layout) + a0_ref[blk * N_EXPERTS + e] + c * SLAB,
                             ROWS_BF16)
        return pltpu.make_async_copy(slab_ref.at[e, c], xe_ref.at[pl.ds(row, SLAB)], sems.at[e, c])

    def drain_block(blk):
        for e in range(N_EXPERTS):
            def drain(c, carry, e=e):
                copy(e, c, blk).wait()
                return carry

            lax.fori_loop(0, nch_ref[blk * N_EXPERTS + e], drain, 0)

    prev_pending = b > 0
    for (last_b, _) in pad_rows:
        prev_pending = jnp.logical_and(prev_pending, b - 1 != last_b)

    @pl.when(prev_pending)
    def _():
        drain_block(b - 1)

    for e in range(N_EXPERTS):
        @pl.when(nchs[e] > 0)
        def _(e=e):
            ye = y0[e * SLAB:(e + 1) * SLAB]
            slab_ref[e, 0, :ROWS_BF16, :] = jnp.where(row16 < r_ref[b * N_EXPERTS + e],
                                                      carry_ref[e], ye[:ROWS_BF16])
            slab_ref[e, 0, ROWS_BF16:, :] = ye[ROWS_BF16:]
            copy(e, 0).start()

            def extra(c, carry):
                pt = _one_hot_rows(pos_ref, e, a0s[e] + c * SLAB, tb)
                slab_ref[e, c] = jnp.dot(pt, h, preferred_element_type=F32).astype(BF16)
                copy(e, c).start()
                return carry

            lax.fori_loop(1, nchs[e], extra, 0)
            o1 = pl.multiple_of(o1_ref[b * N_EXPERTS + e], ROWS_BF16)
            carry_ref[e] = slab_ref[e, c1_ref[b * N_EXPERTS + e], pl.ds(o1, ROWS_BF16), :]

    for (last_b, first_pad) in pad_rows:
        @pl.when(b == last_b)
        def _(last_b=last_b, first_pad=first_pad):
            drain_block(last_b)
            zero_ref[...] = jnp.zeros(zero_ref.shape, BF16)
            cps = []
            for e in range(N_EXPERTS):
                for c in range(PAD // SLAB):
                    r0 = _slot_base(last_b, e, layout) + first_pad + c * SLAB
                    cp = pltpu.make_async_copy(zero_ref, xe_ref.at[pl.ds(r0, SLAB)], sems.at[e, c])
                    cp.start()
                    cps.append(cp)
            for cp in cps:
                cp.wait()


def _dispatch(plan, h2a, pos3, layout, pad_rows, rows):
    n, da = h2a.shape
    grid_spec = pltpu.PrefetchScalarGridSpec(
        num_scalar_prefetch=5,
        grid=(n // TB,),
        in_specs=[
            pl.BlockSpec((TB, da), lambda b, *_: (b, 0)),
            pl.BlockSpec((None, N_EXPERTS, TB), lambda b, *_: (b, 0, 0)),
        ],
        out_specs=pl.BlockSpec(memory_space=pl.ANY),
        scratch_shapes=[
            pltpu.VMEM((N_EXPERTS, MAXCH, SLAB, da), BF16),
            pltpu.VMEM((N_EXPERTS, ROWS_BF16, da), BF16),
            pltpu.VMEM((SLAB, da), BF16),
            pltpu.SemaphoreType.DMA((N_EXPERTS, MAXCH)),
        ],
    )
    return pl.pallas_call(
        functools.partial(_dispatch_kernel, layout=layout, pad_rows=pad_rows),
        grid_spec=grid_spec,
        out_shape=jax.ShapeDtypeStruct((rows, da), BF16),
        compiler_params=_cparams(("arbitrary",)),
        name="moe_dispatch",
    )(*plan, h2a, pos3)


def _ffn_kernel(x_ref, wg_ref, wu_ref, wd_ref, y_ref):
    e = pl.program_id(0)
    d = wg_ref.shape[0]
    x = x_ref[:, :d]
    gl = x_ref[:, d:].astype(F32)
    lane = lax.broadcasted_iota(I32, gl.shape, 1)
    mine = jnp.logical_or(lane == e, lane == e + N_EXPERTS)
    gate = jnp.sum(jnp.where(mine, gl, 0.0), axis=-1, keepdims=True)
    g = jnp.dot(x, wg_ref[...], preferred_element_type=F32)
    u = jnp.dot(x, wu_ref[...], preferred_element_type=F32)
    hid = (g * jax.nn.sigmoid(g) * u).astype(BF16)
    y_ref[...] = (jnp.dot(hid, wd_ref[...], preferred_element_type=F32) * gate).astype(BF16)


def _ffn_tile(cp_tot):
    for tm in (512, 256, 128, 64):
        if cp_tot % tm == 0:
            return tm
    raise ValueError("slot rows per expert must be a multiple of 64")


def _ffn(xe, wg, wu, wd, cp_tot, e0):
    rows, da = xe.shape
    d, ff = wg.shape[1], wg.shape[2]
    tm = _ffn_tile(cp_tot)
    tiles = cp_tot // tm
    return pl.pallas_call(
        _ffn_kernel,
        grid=(N_EXPERTS, tiles),
        in_specs=[
            pl.BlockSpec((tm, da), lambda e, r: (e * tiles + r, 0)),
            pl.BlockSpec((None, d, ff), lambda e, r: (e + e0, 0, 0), pipeline_mode=pl.Buffered(1)),
            pl.BlockSpec((None, d, ff), lambda e, r: (e + e0, 0, 0), pipeline_mode=pl.Buffered(1)),
            pl.BlockSpec((None, ff, d), lambda e, r: (e + e0, 0, 0), pipeline_mode=pl.Buffered(1)),
        ],
        out_specs=pl.BlockSpec((tm, d), lambda e, r: (e * tiles + r, 0)),
        out_shape=jax.ShapeDtypeStruct((rows, d), BF16),
        compiler_params=_cparams(("arbitrary", "arbitrary")),
        name="expert_ffn",
    )(xe, wg, wu, wd)


def _combine_kernel(a0_ref, nch_ref, ye_ref, pos_ref, x_ref, gt_ref, fg_ref, o_ref,
                    win_ref, xwin_ref, acc_ref, sems, xsems, *, layout, final, first_block):
    i = pl.program_id(0)
    b = i + first_block
    slot = i % 2
    tb, d = x_ref.shape
    dn = (((0,), (0,)), ((), ()))
    a0s = [a0_ref[b * N_EXPERTS + e] for e in range(N_EXPERTS)]
    nchs = [nch_ref[b * N_EXPERTS + e] for e in range(N_EXPERTS)]

    def row_of(blk, e, c):
        return pl.multiple_of(_slot_base(blk, e, layout) + a0_ref[blk * N_EXPERTS + e] + c * SLAB,
                              ROWS_BF16)

    def first_copy(blk, e, buf):
        return pltpu.make_async_copy(ye_ref.at[pl.ds(row_of(blk, e, 0), SLAB)],
                                     win_ref.at[buf, pl.ds(e * SLAB, SLAB)], sems.at[buf, e])

    def extra_copy(e, c):
        return pltpu.make_async_copy(ye_ref.at[pl.ds(row_of(b, e, c), SLAB)], xwin_ref.at[e, c - 1],
                                     xsems.at[e, c - 1])

    @pl.when(i == 0)
    def _():
        for e in range(N_EXPERTS):
            first_copy(b, e, 0).start()

    @pl.when(i + 1 < pl.num_programs(0))
    def _():
        for e in range(N_EXPERTS):
            first_copy(b + 1, e, 1 - slot).start()

    for e in range(N_EXPERTS):
        def fetch(c, carry, e=e):
            extra_copy(e, c).start()
            return carry

        lax.fori_loop(1, nchs[e], fetch, 0)

    pt = _one_hot_stack(pos_ref, a0s, tb)
    for e in range(N_EXPERTS):
        first_copy(b, e, slot).wait()
    acc_ref[...] = lax.dot_general(pt, win_ref[slot], dn, preferred_element_type=F32)

    for e in range(N_EXPERTS):
        def extra(c, carry, e=e):
            extra_copy(e, c).wait()
            ptc = _one_hot_rows(pos_ref, e, a0s[e] + c * SLAB, tb)
            acc_ref[...] += lax.dot_general(ptc, xwin_ref[e, c - 1], dn, preferred_element_type=F32)
            return carry

        lax.fori_loop(1, nchs[e], extra, 0)

    x2 = x_ref[...] + gt_ref[...] * acc_ref[...]
    if final:
        x2 = x2 * lax.rsqrt(jnp.mean(x2 * x2, axis=-1, keepdims=True) + EPS) * fg_ref[...]
    o_ref[...] = x2


def _combine(a0, nch, ye, pos3, x1, mod3, final_g, layout, final, first_block, nblocks):
    d = x1.shape[1]
    tps = SEQ // TB
    fb = first_block
    grid_spec = pltpu.PrefetchScalarGridSpec(
        num_scalar_prefetch=2,
        grid=(nblocks,),
        in_specs=[
            pl.BlockSpec(memory_space=pl.ANY),
            pl.BlockSpec((None, N_EXPERTS, TB), lambda i, *_: (i + fb, 0, 0)),
            pl.BlockSpec((TB, d), lambda i, *_: (i + fb, 0)),
            pl.BlockSpec((None, 1, d), lambda i, *_: (((i + fb) // tps) * N_MOD + 5, 0, 0)),
            pl.BlockSpec((1, d), lambda i, *_: (0, 0)),
        ],
        out_specs=pl.BlockSpec((TB, d), lambda i, *_: (i, 0)),
        scratch_shapes=[
            pltpu.VMEM((2, N_EXPERTS * SLAB, d), BF16),
            pltpu.VMEM((N_EXPERTS, MAXCH - 1, SLAB, d), BF16),
            pltpu.VMEM((TB, d), F32),
            pltpu.SemaphoreType.DMA((2, N_EXPERTS)),
            pltpu.SemaphoreType.DMA((N_EXPERTS, MAXCH - 1)),
        ],
    )
    return pl.pallas_call(
        functools.partial(_combine_kernel, layout=layout, final=final, first_block=fb),
        grid_spec=grid_spec,
        out_shape=jax.ShapeDtypeStruct((nblocks * TB, d), F32),
        compiler_params=_cparams(("arbitrary",)),
        name="moe_combine",
    )(a0, nch, ye, pos3, x1, mod3, final_g)


def _rope_tables():
    half = ROT_DIM // 2
    inv = ROPE_THETA ** (-jnp.arange(0, ROT_DIM, 2, dtype=F32) / ROT_DIM)
    ang = jnp.arange(SEQ, dtype=F32)[:, None] * inv[None, :]
    cos, sin = jnp.cos(ang), jnp.sin(ang)
    dd = jnp.arange(LANES) % HEAD_DIM
    cos_l = jnp.where(dd[None, :] < ROT_DIM, cos[:, dd % half], 1.0)
    sin_l = sin[:, dd % half]
    sa = jnp.where((dd[None, :] >= half) & (dd[None, :] < ROT_DIM), sin_l, 0.0)
    sb = jnp.where(dd[None, :] < half, -sin_l, 0.0)
    return cos_l.astype(F32), sa.astype(F32), sb.astype(F32)


def _trunk(xs, cs, norm1_g, w_ada, b_ada, w_in, b_gate, lam, subln_g, gmlp_ln_g, gmlp_ln_b,
           w_spatial, b_spatial, w_br_attn, w_br_gmlp, w_out, norm2_g, w_router,
           w_e_gate, w_e_up, w_e_down, final_g):
    d = D_MODEL
    assert N_HEADS * 2 * HEAD_DIM == d and GMLP_GROUPS * LANES == d and CHUNK == LANES
    assert SEQ % TB == 0 and TB % CHUNK == 0
    batches = [x.shape[0] for x in xs]
    bt = sum(batches)
    n = bt * SEQ
    n_a = batches[0] * SEQ
    xa, xb, xb_row0 = xs[0].reshape(-1, d), xs[1].reshape(-1, d), 0
    c = jnp.concatenate(cs, axis=0)
    bp = -(-bt // 8) * 8
    c_pad = jnp.pad(c, ((0, bp - bt), (0, 0)))
    mod = _ada(c_pad, w_ada, b_ada)

    caps = [CAPACITY_FACTOR * b * SEQ // N_EXPERTS for b in batches]
    nbs = [b * SEQ // TB for b in batches]
    cps = [cap + PAD for cap in caps]
    cp_tot = sum(cps)
    layout = (nbs[0], cps[0], cp_tot)
    groups = ((0, nbs[0], caps[0]), (nbs[0], nbs[1], caps[1]))
    pad_rows = ((nbs[0] - 1, caps[0]), (nbs[0] + nbs[1] - 1, caps[1]))
    rows = N_EXPERTS * cp_tot
    nblk = n // TB

    cos_t, sa_t, sb_t = _rope_tables()
    ff = w_e_gate.shape[-1]
    w_in_b = _to_bf16(w_in, row_block=d // 4)
    w_eg_b = _to_bf16(w_e_gate.reshape(DEPTH * N_EXPERTS, d, ff))
    w_eu_b = _to_bf16(w_e_up.reshape(DEPTH * N_EXPERTS, d, ff))
    w_ed_b = _to_bf16(w_e_down.reshape(DEPTH * N_EXPERTS, ff, d))
    w_ba_b, w_bg_b, w_o_b = _to_bf16(w_br_attn), _to_bf16(w_br_gmlp), _to_bf16(w_out)
    w_sp_b = _to_bf16(w_spatial.reshape(DEPTH * GMLP_GROUPS, CHUNK, CHUNK))
    outs = None
    for l in range(DEPTH):
        lam_init = 0.8 - 0.6 * math.exp(-0.3 * l)
        mod3 = mod[l, :bt].reshape(bt * N_MOD, 1, d)
        proj = _inproj(xa, xb, n, n_a, xb_row0, mod3, norm1_g[l][None], w_in_b, l,
                       b_gate[l][None], cos_t, sa_t, sb_t)
        attn = _attention(proj.reshape(bt, SEQ, N_SEG * d), lam[l], subln_g[l][None], lam_init)
        wr = jnp.pad(jnp.concatenate([w_router[l], w_router[l]], axis=1),
                     ((0, 0), (0, LANES - 2 * N_EXPERTS)))
        wrh = wr.astype(BF16)
        wrl = (wr - wrh.astype(F32)).astype(BF16)
        x1, h2a, aff = _mix(proj, attn.reshape(n, d), xa, xb, n_a, xb_row0, mod3,
                            gmlp_ln_g[l][None], gmlp_ln_b[l][None],
                            w_sp_b[l * GMLP_GROUPS:(l + 1) * GMLP_GROUPS], b_spatial[l].T,
                            w_ba_b[l], w_bg_b[l], w_o_b[l], norm2_g[l][None], wrh, wrl)
        bits3 = lax.bitcast_convert_type(aff, I32).reshape(nblk, TB, N_EXPERTS).transpose(0, 2, 1)
        pos3, s0_t, cnt_t = _route(bits3, groups)
        plan = _window_plan(s0_t[:, :nblk].T.reshape(-1), cnt_t[:, :nblk].T.reshape(-1))
        xe = _dispatch(plan, h2a, pos3, layout, pad_rows, rows)
        ye = _ffn(xe, w_eg_b, w_eu_b, w_ed_b, cp_tot, l * N_EXPERTS)
        if l < DEPTH - 1:
            x = _combine(plan[0], plan[2], ye, pos3, x1, mod3, final_g[None], layout, False,
                         0, nblk)
            xa, xb, xb_row0 = x, x, n_a
        else:
            outs = tuple(
                _combine(plan[0], plan[2], ye, pos3, x1, mod3, final_g[None], layout, True,
                         fb, nb).reshape(bsz, SEQ, d)
                for fb, nb, bsz in ((0, nbs[0], batches[0]), (nbs[0], nbs[1], batches[1])))
    return outs


def kernel(x_prompt, x_sample, c_prompt, c_sample, norm1_g, w_ada, b_ada, w_in, b_gate, lam, subln_g, gmlp_ln_g, gmlp_ln_b, w_spatial, b_spatial, w_br_attn, w_br_gmlp, w_out, norm2_g, w_router, w_e_gate, w_e_up, w_e_down, final_g):
    return _trunk((x_prompt, x_sample), (c_prompt, c_sample), norm1_g, w_ada, b_ada, w_in,
                  b_gate, lam, subln_g, gmlp_ln_g, gmlp_ln_b, w_spatial, b_spatial, w_br_attn,
                  w_br_gmlp, w_out, norm2_g, w_router, w_e_gate, w_e_up, w_e_down, final_g)
```

```python
import functools
import math

import jax
import jax.numpy as jnp
from jax import lax
from jax.experimental import pallas as pl
from jax.experimental.pallas import tpu as pltpu

F32 = jnp.float32
BF16 = jnp.bfloat16
I32 = jnp.int32

D_MODEL = 1024
SEQ = 4096
DEPTH = 2
N_HEADS = 8
HEAD_DIM = 64
ROT_DIM = HEAD_DIM // 4
ROPE_THETA = 500000.0
CHUNK = 128
GMLP_GROUPS = 8
N_EXPERTS = 16
EXPERT_FF = 2048
CAPACITY_FACTOR = 2
N_MOD = 6
N_SEG = 7
EPS = 1e-6

LANES = 128
VMEM_LIMIT = 56 * 1024 * 1024

TM_PROJ = 1024
NC_PROJ = 256
TQ = 256
NSUB_Q = 4
TK = 1024
SCORE_BOUND = 60.0
TM_MIX = 256
TB = 256
SLAB = 64
ROWS_BF16 = 16
MAXCH = -(-(TB + ROWS_BF16 - 1) // SLAB)
PAD = TB


def _cparams(sem):
    return pltpu.CompilerParams(dimension_semantics=sem, vmem_limit_bytes=VMEM_LIMIT)


def _ada_kernel(c_ref, w_ref, b_ref, o_ref):
    c = c_ref[...]
    a = c * jax.nn.sigmoid(c)
    o_ref[0] = jnp.dot(a, w_ref[0], preferred_element_type=F32,
                       precision=lax.Precision.HIGHEST) + b_ref[0]


def _ada(c_pad, w_ada, b_ada):
    bp, d = c_pad.shape
    depth = w_ada.shape[0]
    return pl.pallas_call(
        _ada_kernel,
        grid=(depth, N_MOD),
        in_specs=[
            pl.BlockSpec((bp, d), lambda l, j: (0, 0)),
            pl.BlockSpec((1, d, d), lambda l, j: (l, 0, j)),
            pl.BlockSpec((1, 1, d), lambda l, j: (l, 0, j)),
        ],
        out_specs=pl.BlockSpec((1, bp, d), lambda l, j: (l, 0, j)),
        out_shape=jax.ShapeDtypeStruct((depth, bp, N_MOD * d), F32),
        compiler_params=_cparams(("arbitrary", "arbitrary")),
        name="ada_mod",
    )(c_pad, w_ada, b_ada.reshape(depth, 1, N_MOD * d))


def _gelu(x):
    return jax.nn.gelu(x)


def _cast_kernel(x_ref, o_ref):
    o_ref[...] = x_ref[...].astype(o_ref.dtype)


def _to_bf16(w, row_block=None):
    lead, r, c = w.shape
    rb = r if row_block is None else row_block
    spec = pl.BlockSpec((None, rb, c), lambda i, k: (i, k, 0))
    return pl.pallas_call(
        _cast_kernel,
        grid=(lead, r // rb),
        in_specs=[spec],
        out_specs=spec,
        out_shape=jax.ShapeDtypeStruct(w.shape, BF16),
        compiler_params=_cparams(("arbitrary", "arbitrary")),
        name="to_bf16",
    )(w)


def _inproj_kernel(xa_ref, xb_ref, sh_ref, sc_ref, g_ref, w_ref, bg_ref, cos_ref, sa_ref, sb_ref,
                   o_ref, h_ref, *, tiles_a):
    j = pl.program_id(1)

    @pl.when(j == 0)
    def _():
        x = jnp.where(pl.program_id(0) < tiles_a, xa_ref[...], xb_ref[...])
        r = x * lax.rsqrt(jnp.mean(x * x, axis=-1, keepdims=True) + EPS) * g_ref[...]
        h_ref[...] = (r * (1.0 + sc_ref[...]) + sh_ref[...]).astype(BF16)

    def project(epilogue):
        h = h_ref[...]
        for c in range(w_ref.shape[1] // NC_PROJ):
            cols = slice(c * NC_PROJ, (c + 1) * NC_PROJ)
            acc = jnp.dot(h, w_ref[:, cols], preferred_element_type=F32)
            o_ref[:, cols] = epilogue(acc, cols).astype(BF16)

    @pl.when(j < 2)
    def _():
        scale = jnp.where(j == 0, HEAD_DIM ** -0.5 * math.log2(math.e), 1.0).astype(F32)
        cos, sa, sb = cos_ref[...] * scale, sa_ref[...] * scale, sb_ref[...] * scale
        half = ROT_DIM // 2

        def rope(acc, cols):
            parts = []
            for c in range(acc.shape[1] // LANES):
                a = acc[:, c * LANES:(c + 1) * LANES]
                parts.append(a * cos + pltpu.roll(a, half, 1) * sa
                             + pltpu.roll(a, LANES - half, 1) * sb)
            return jnp.concatenate(parts, axis=-1)

        project(rope)

    @pl.when(j == 2)
    def _():
        project(lambda acc, cols: acc)

    @pl.when(jnp.logical_or(j == 3, j == 4))
    def _():
        project(lambda acc, cols: _gelu(acc))

    @pl.when(j >= 5)
    def _():
        project(lambda acc, cols: jax.nn.sigmoid(acc + bg_ref[:, cols]))


def _two_group_specs(rows, d, tm, rows_a, xb_row0):
    ta, tb0 = rows_a // tm, xb_row0 // tm
    return ta, [pl.BlockSpec((tm, d), lambda i, *_: (jnp.minimum(i, ta - 1), 0)),
                pl.BlockSpec((tm, d), lambda i, *_: (jnp.maximum(i - ta, 0) + tb0, 0))]


def _inproj(xa, xb, n, rows_a, xb_row0, mod3, norm_g, w_in, layer, b_gate, cos_t, sa_t, sb_t):
    d = xa.shape[1]
    tm = min(TM_PROJ, SEQ)
    tiles_per_seq = SEQ // tm
    tiles_a, xspecs = _two_group_specs(n, d, tm, rows_a, xb_row0)
    return pl.pallas_call(
        functools.partial(_inproj_kernel, tiles_a=tiles_a),
        grid=(n // tm, N_SEG),
        in_specs=xspecs + [
            pl.BlockSpec((None, 1, d), lambda i, j: ((i // tiles_per_seq) * N_MOD + 0, 0, 0)),
            pl.BlockSpec((None, 1, d), lambda i, j: ((i // tiles_per_seq) * N_MOD + 1, 0, 0)),
            pl.BlockSpec((1, d), lambda i, j: (0, 0)),
            pl.BlockSpec((None, d, d), lambda i, j: (layer, 0, j)),
            pl.BlockSpec((1, d), lambda i, j: (0, jnp.maximum(j - 5, 0))),
            pl.BlockSpec((tm, LANES), lambda i, j: (i % tiles_per_seq, 0)),
            pl.BlockSpec((tm, LANES), lambda i, j: (i % tiles_per_seq, 0)),
            pl.BlockSpec((tm, LANES), lambda i, j: (i % tiles_per_seq, 0)),
        ],
        out_specs=pl.BlockSpec((tm, d), lambda i, j: (i, j)),
        out_shape=jax.ShapeDtypeStruct((n, N_SEG * d), BF16),
        scratch_shapes=[pltpu.VMEM((tm, d), BF16)],
        compiler_params=_cparams(("arbitrary", "arbitrary")),
        name="in_proj",
    )(xa, xb, mod3, mod3, norm_g, w_in, b_gate, cos_t, sa_t, sb_t)


def _attn_kernel(q_ref, k_ref, v_ref, lam_ref, sg_ref, o_ref, vaug_ref, p_ref, kn_ref, *,
                 lam_init):
    tqt, vd = q_ref.shape
    s_len = k_ref.shape[0]
    tk = min(TK, s_len)
    tq = min(TQ, tqt)
    nsub = tqt // tq
    nch = s_len // tk

    @pl.when(pl.program_id(2) == 0)
    def _():
        vaug_ref[:, :vd] = v_ref[...]
        vaug_ref[:, vd:] = jnp.ones((s_len, vd), BF16)
        kf = k_ref[...].astype(F32)
        kn_ref[0] = jnp.max(jnp.sum(kf * kf, axis=-1, keepdims=True))

    qf = q_ref[...].astype(F32)
    qn = jnp.max(jnp.sum(qf * qf, axis=-1, keepdims=True))
    dn = (((1,), (1,)), ((), ()))
    lp = lam_ref[...]
    lam = (jnp.exp(jnp.sum(lp[0:1] * lp[1:2], axis=-1, keepdims=True))
           - jnp.exp(jnp.sum(lp[2:3] * lp[3:4], axis=-1, keepdims=True)) + lam_init)

    def stacked(q):
        lane = lax.broadcasted_iota(I32, q.shape, 1)
        zero = jnp.zeros_like(q)
        return jnp.concatenate([jnp.where(lane < HEAD_DIM, q, zero),
                                jnp.where(lane >= HEAD_DIM, q, zero)], axis=0)

    def sub_ln(o):
        r = o * lax.rsqrt(jnp.mean(o * o, axis=-1, keepdims=True) + EPS)
        return ((r * sg_ref[...]) * (1.0 - lam_init)).astype(BF16)

    small = qn * kn_ref[0] <= SCORE_BOUND * SCORE_BOUND

    @pl.when(small)
    def _():
        prev = None
        for t in range(nsub + 1):
            if t < nsub:
                qq = stacked(q_ref[t * tq:(t + 1) * tq, :])
                l = jnp.zeros((2 * tq, 1), F32)
            if prev is not None:
                pt, pl_sum = prev
                a1 = lam * pl_sum[:tq] / pl_sum[tq:]
                acc = jnp.zeros((tq, vd), F32)
            for c in range(nch):
                ks = slice(c * tk, (c + 1) * tk)
                if t < nsub:
                    p = jnp.exp2(lax.dot_general(qq, k_ref[ks, :], dn, preferred_element_type=F32))
                    p_ref[t % 2, :, ks] = p
                    l = l + jnp.sum(p, axis=-1, keepdims=True)
                if prev is not None:
                    pd = p_ref[pt % 2, :tq, ks] - p_ref[pt % 2, tq:, ks] * a1
                    acc = acc + jnp.dot(pd.astype(BF16), v_ref[ks, :],
                                        preferred_element_type=F32)
            if prev is not None:
                o_ref[pt * tq:(pt + 1) * tq, :] = sub_ln(acc / pl_sum[:tq])
            prev = (t, l) if t < nsub else None

    @pl.when(jnp.logical_not(small))
    def _():
        def tile(t, carry):
            rows = pl.ds(pl.multiple_of(t * tq, tq), tq)
            qq = stacked(q_ref[rows, :])
            m = jnp.full((2 * tq, 1), -jnp.inf, F32)
            acc = jnp.zeros((2 * tq, 2 * vd), F32)
            for c in range(nch):
                ks = slice(c * tk, (c + 1) * tk)
                s = lax.dot_general(qq, k_ref[ks, :], dn, preferred_element_type=F32)
                m_new = jnp.maximum(m, jnp.max(s, axis=-1, keepdims=True))
                alpha = jnp.exp2(m - m_new)
                p = jnp.exp2(s - m_new).astype(BF16)
                acc = alpha * acc + jnp.dot(p, vaug_ref[ks, :], preferred_element_type=F32)
                m = m_new
            on = acc[:, :vd] / acc[:, vd:]
            o_ref[rows, :] = sub_ln(on[:tq] - lam * on[tq:])
            return carry

        lax.fori_loop(0, nsub, tile, 0)


def _attention(proj3, lam, subln_g, lam_init):
    bt, s, _ = proj3.shape
    vd = 2 * HEAD_DIM
    tq = min(TQ, s)
    tqt = min(TQ * NSUB_Q, s)
    return pl.pallas_call(
        functools.partial(_attn_kernel, lam_init=lam_init),
        grid=(bt, N_HEADS, s // tqt),
        in_specs=[
            pl.BlockSpec((None, tqt, vd), lambda b, h, i: (b, i, h)),
            pl.BlockSpec((None, s, vd), lambda b, h, i: (b, 0, N_HEADS + h)),
            pl.BlockSpec((None, s, vd), lambda b, h, i: (b, 0, 2 * N_HEADS + h)),
            pl.BlockSpec((4, HEAD_DIM), lambda b, h, i: (0, 0)),
            pl.BlockSpec((1, vd), lambda b, h, i: (0, 0)),
        ],
        out_specs=pl.BlockSpec((None, tqt, vd), lambda b, h, i: (b, i, h)),
        out_shape=jax.ShapeDtypeStruct((bt, s, N_HEADS * vd), BF16),
        scratch_shapes=[pltpu.VMEM((s, 2 * vd), BF16), pltpu.VMEM((2, 2 * tq, s), F32),
                        pltpu.SMEM((1,), F32)],
        compiler_params=_cparams(("arbitrary", "arbitrary", "arbitrary")),
        name="diff_attn",
    )(proj3, proj3, proj3, lam, subln_g)


def _mix_kernel(u_ref, vg_ref, ga_ref, gg_ref, at_ref, xa_ref, xb_ref, lng_ref, lnb_ref, ws_ref,
                bs_ref, wba_ref, wbg_ref, wo_ref, gt1_ref, n2g_ref, sh2_ref, sc2_ref, wrh_ref,
                wrl_ref, x1_ref, h2_ref, aff_ref, *, tiles_a):
    tm = u_ref.shape[0]
    vg = vg_ref[...].astype(F32)
    mu = jnp.mean(vg, axis=-1, keepdims=True)
    xc = vg - mu
    var = jnp.mean(xc * xc, axis=-1, keepdims=True)
    vn = (xc * lax.rsqrt(var + EPS) * lng_ref[...] + lnb_ref[...]).astype(BF16)
    bs = bs_ref[...]
    cols = []
    for g in range(GMLP_GROUPS):
        rows = []
        for c in range(tm // CHUNK):
            blk = vn[c * CHUNK:(c + 1) * CHUNK, g * LANES:(g + 1) * LANES]
            rows.append(jnp.dot(ws_ref[g], blk, preferred_element_type=F32) + bs[:, g:g + 1])
        cols.append(jnp.concatenate(rows, axis=0))
    mixed = jnp.concatenate(cols, axis=1)
    gm = (u_ref[...].astype(F32) * mixed).astype(BF16)
    o_g = jnp.dot(gm, wbg_ref[...], preferred_element_type=F32)
    o_a = jnp.dot(at_ref[...], wba_ref[...], preferred_element_type=F32)
    merged = (ga_ref[...].astype(F32) * o_a + gg_ref[...].astype(F32) * o_g).astype(BF16)
    mo = jnp.dot(merged, wo_ref[...], preferred_element_type=F32)
    x_in = jnp.where(pl.program_id(0) < tiles_a, xa_ref[...], xb_ref[...])
    x1 = x_in + gt1_ref[...] * mo
    x1_ref[...] = x1
    r = x1 * lax.rsqrt(jnp.mean(x1 * x1, axis=-1, keepdims=True) + EPS) * n2g_ref[...]
    h2 = r * (1.0 + sc2_ref[...]) + sh2_ref[...]
    h2h = h2.astype(BF16)
    d = h2.shape[1]
    h2_ref[:, :d] = h2h
    h2l = (h2 - h2h.astype(F32)).astype(BF16)
    wrh = wrh_ref[...]
    logits = (jnp.dot(h2h, wrh, preferred_element_type=F32)
              + jnp.dot(h2l, wrh, preferred_element_type=F32)
              + jnp.dot(h2h, wrl_ref[...], preferred_element_type=F32))
    lane = lax.broadcasted_iota(I32, logits.shape, 1)
    first = lane < N_EXPERTS
    m = jnp.max(jnp.where(first, logits, -jnp.inf), axis=-1, keepdims=True)
    p = jnp.exp(logits - m)
    aff = p / jnp.sum(jnp.where(first, p, 0.0), axis=-1, keepdims=True)
    aff_ref[...] = aff[:, :N_EXPERTS]
    hi = aff.astype(BF16)
    lo = (aff - hi.astype(F32)).astype(BF16)
    h2_ref[:, d:] = jnp.where(first, hi, jnp.where(lane < 2 * N_EXPERTS, lo, jnp.zeros_like(lo)))


def _mix(proj, attn, xa, xb, rows_a, xb_row0, mod3, lng, lnb, ws, bs_t, wba, wbg, wo, n2g, wrh,
         wrl):
    n, d = attn.shape
    tm = min(TM_MIX, SEQ)
    tps = SEQ // tm
    tiles_a, xspecs = _two_group_specs(n, d, tm, rows_a, xb_row0)
    row = lambda i: (i, 0)
    full2 = lambda i: (0, 0)

    def seg(k):
        return pl.BlockSpec((tm, d), lambda i: (i, k))

    def modspec(k):
        return pl.BlockSpec((None, 1, d), lambda i: ((i // tps) * N_MOD + k, 0, 0))

    return pl.pallas_call(
        functools.partial(_mix_kernel, tiles_a=tiles_a),
        grid=(n // tm,),
        in_specs=[
            seg(3), seg(4), seg(5), seg(6),
            pl.BlockSpec((tm, d), row),
        ] + xspecs + [
            pl.BlockSpec((1, d), full2),
            pl.BlockSpec((1, d), full2),
            pl.BlockSpec((GMLP_GROUPS, CHUNK, CHUNK), lambda i: (0, 0, 0)),
            pl.BlockSpec((CHUNK, GMLP_GROUPS), full2),
            pl.BlockSpec((d, d), full2),
            pl.BlockSpec((d, d), full2),
            pl.BlockSpec((d, d), full2),
            modspec(2),
            pl.BlockSpec((1, d), full2),
            modspec(3),
            modspec(4),
            pl.BlockSpec((d, LANES), full2),
            pl.BlockSpec((d, LANES), full2),
        ],
        out_specs=[
            pl.BlockSpec((tm, d), row),
            pl.BlockSpec((tm, d + LANES), row),
            pl.BlockSpec((tm, N_EXPERTS), row),
        ],
        out_shape=[
            jax.ShapeDtypeStruct((n, d), F32),
            jax.ShapeDtypeStruct((n, d + LANES), BF16),
            jax.ShapeDtypeStruct((n, N_EXPERTS), F32),
        ],
        compiler_params=_cparams(("arbitrary",)),
        name="branch_mix",
    )(proj, proj, proj, proj, attn, xa, xb, lng, lnb, ws, bs_t, wba, wbg, wo, mod3, n2g, mod3,
      mod3, wrh, wrl)


def _route_kernel(bits_ref, pos_ref, s0_ref, cnt_ref, *, groups):
    ne, tb = bits_ref.shape[1], bits_ref.shape[2]
    ri = lax.broadcasted_iota(I32, (tb, tb), 0)
    ci = lax.broadcasted_iota(I32, (tb, tb), 1)
    tri = jnp.where(ri < ci, 1.0, 0.0).astype(BF16)
    blk_lane = lax.broadcasted_iota(I32, s0_ref.shape, 1)
    s0_ref[...] = jnp.zeros(s0_ref.shape, I32)
    cnt_ref[...] = jnp.zeros(cnt_ref.shape, I32)

    for (b0, nb, cap) in groups:
        def count(pred_fn):
            def body(i, acc):
                return acc + jnp.where(pred_fn(bits_ref[b0 + i]), 1, 0)
            acc = lax.fori_loop(0, nb, body, jnp.zeros((ne, tb), I32))
            return jnp.sum(acc, axis=1, keepdims=True)

        def bs_body(k, thr):
            cand = thr | lax.shift_left(jnp.int32(1), jnp.int32(30) - k)
            c = count(lambda blk: blk >= cand)
            return jnp.where(c >= cap, cand, thr)

        thr = lax.fori_loop(0, 31, bs_body, jnp.zeros((ne, 1), I32))
        n_gt = count(lambda blk: blk > thr)
        need = (cap - n_gt).astype(F32)

        def scan_body(i, carry):
            ceq, cpos = carry
            blk = bits_ref[b0 + i]
            gt = blk > thr
            eq = blk == thr
            eqf = jnp.where(eq, 1.0, 0.0)
            eq_excl = jnp.dot(eqf.astype(BF16), tri, preferred_element_type=F32) + ceq
            self = jnp.where(gt, 1.0, jnp.where(eq_excl < need, eqf, 0.0))
            pos_excl = jnp.dot(self.astype(BF16), tri, preferred_element_type=F32) + cpos
            pos_ref[b0 + i] = jnp.where(self > 0.5, pos_excl.astype(I32), -1)
            n_sel = jnp.sum(self, axis=1, keepdims=True)
            hit = blk_lane == (b0 + i)
            s0_ref[...] = jnp.where(hit, cpos.astype(I32), s0_ref[...])
            cnt_ref[...] = jnp.where(hit, n_sel.astype(I32), cnt_ref[...])
            return (ceq + jnp.sum(eqf, axis=1, keepdims=True), cpos + n_sel)

        zero = jnp.zeros((ne, 1), F32)
        lax.fori_loop(0, nb, scan_body, (zero, zero))


def _route(bits3, groups):
    nblk, ne, tb = bits3.shape
    nbp = -(-nblk // LANES) * LANES
    return pl.pallas_call(
        functools.partial(_route_kernel, groups=groups),
        out_shape=[
            jax.ShapeDtypeStruct((nblk, ne, tb), I32),
            jax.ShapeDtypeStruct((ne, nbp), I32),
            jax.ShapeDtypeStruct((ne, nbp), I32),
        ],
        compiler_params=pltpu.CompilerParams(vmem_limit_bytes=VMEM_LIMIT),
        name="route_topc",
    )(bits3)


def _slot_base(b, e, layout):
    nb0, cp0, cp_tot = layout
    return e * cp_tot + jnp.where(b >= nb0, cp0, 0)


def _one_hot_rows(pos_ref, e, first_slot, tb):
    jrow = lax.broadcasted_iota(I32, (SLAB, tb), 0)
    return jnp.where(jrow == pos_ref[e:e + 1, :] - first_slot, 1.0, 0.0).astype(BF16)


def _one_hot_stack(pos_ref, first_slots, tb):
    return jnp.concatenate([_one_hot_rows(pos_ref, e, first_slots[e], tb)
                            for e in range(N_EXPERTS)], axis=0)


def _window_plan(s0, cnt):
    a0 = (s0 // ROWS_BF16) * ROWS_BF16
    r = s0 - a0
    nch = jnp.where(cnt > 0, (r + cnt + SLAB - 1) // SLAB, 0)
    nxt = ((s0 + cnt) // ROWS_BF16) * ROWS_BF16 - a0
    c1 = jnp.minimum(nxt // SLAB, jnp.maximum(nch - 1, 0))
    o1 = jnp.where(nxt // SLAB == c1, nxt % SLAB, SLAB - ROWS_BF16)
    return a0, r, nch, c1, o1


def _dispatch_kernel(a0_ref, r_ref, nch_ref, c1_ref, o1_ref, h_ref, pos_ref, xe_ref,
                     slab_ref, carry_ref, zero_ref, sems, *, layout, pad_rows):
    b = pl.program_id(0)
    tb, da = h_ref.shape

    @pl.when(b == 0)
    def _():
        carry_ref[...] = jnp.zeros(carry_ref.shape, BF16)

    h = h_ref[...]
    a0s = [a0_ref[b * N_EXPERTS + e] for e in range(N_EXPERTS)]
    nchs = [nch_ref[b * N_EXPERTS + e] for e in range(N_EXPERTS)]
    y0 = jnp.dot(_one_hot_stack(pos_ref, a0s, tb), h, preferred_element_type=F32).astype(BF16)
    row16 = lax.broadcasted_iota(I32, (ROWS_BF16, da), 0)

    def copy(e, c, blk=b):
        row = pl.multiple_of(_slot_base(blk, e, layout) + a0_ref[blk * N_EXPERTS + e] + c * SLAB,
                             ROWS_BF16)
        return pltpu.make_async_copy(slab_ref.at[e, c], xe_ref.at[pl.ds(row, SLAB)], sems.at[e, c])

    def drain_block(blk):
        for e in range(N_EXPERTS):
            def drain(c, carry, e=e):
                copy(e, c, blk).wait()
                return carry

            lax.fori_loop(0, nch_ref[blk * N_EXPERTS + e], drain, 0)

    prev_pending = b > 0
    for (last_b, _) in pad_rows:
        prev_pending = jnp.logical_and(prev_pending, b - 1 != last_b)

    @pl.when(prev_pending)
    def _():
        drain_block(b - 1)

    for e in range(N_EXPERTS):
        @pl.when(nchs[e] > 0)
        def _(e=e):
            ye = y0[e * SLAB:(e + 1) * SLAB]
            slab_ref[e, 0, :ROWS_BF16, :] = jnp.where(row16 < r_ref[b * N_EXPERTS + e],
                                                      carry_ref[e], ye[:ROWS_BF16])
            slab_ref[e, 0, ROWS_BF16:, :] = ye[ROWS_BF16:]
            copy(e, 0).start()

            def extra(c, carry):
                pt = _one_hot_rows(pos_ref, e, a0s[e] + c * SLAB, tb)
                slab_ref[e, c] = jnp.dot(pt, h, preferred_element_type=F32).astype(BF16)
                copy(e, c).start()
                return carry

            lax.fori_loop(1, nchs[e], extra, 0)
            o1 = pl.multiple_of(o1_ref[b * N_EXPERTS + e], ROWS_BF16)
            carry_ref[e] = slab_ref[e, c1_ref[b * N_EXPERTS + e], pl.ds(o1, ROWS_BF16), :]

    for (last_b, first_pad) in pad_rows:
        @pl.when(b == last_b)
        def _(last_b=last_b, first_pad=first_pad):
            drain_block(last_b)
            zero_ref[...] = jnp.zeros(zero_ref.shape, BF16)
            cps = []
            for e in range(N_EXPERTS):
                for c in range(PAD // SLAB):
                    r0 = _slot_base(last_b, e, layout) + first_pad + c * SLAB
                    cp = pltpu.make_async_copy(zero_ref, xe_ref.at[pl.ds(r0, SLAB)], sems.at[e, c])
                    cp.start()
                    cps.append(cp)
            for cp in cps:
                cp.wait()


def _dispatch(plan, h2a, pos3, layout, pad_rows, rows):
    n, da = h2a.shape
    grid_spec = pltpu.PrefetchScalarGridSpec(
        num_scalar_prefetch=5,
        grid=(n // TB,),
        in_specs=[
            pl.BlockSpec((TB, da), lambda b, *_: (b, 0)),
            pl.BlockSpec((None, N_EXPERTS, TB), lambda b, *_: (b, 0, 0)),
        ],
        out_specs=pl.BlockSpec(memory_space=pl.ANY),
        scratch_shapes=[
            pltpu.VMEM((N_EXPERTS, MAXCH, SLAB, da), BF16),
            pltpu.VMEM((N_EXPERTS, ROWS_BF16, da), BF16),
            pltpu.VMEM((SLAB, da), BF16),
            pltpu.SemaphoreType.DMA((N_EXPERTS, MAXCH)),
        ],
    )
    return pl.pallas_call(
        functools.partial(_dispatch_kernel, layout=layout, pad_rows=pad_rows),
        grid_spec=grid_spec,
        out_shape=jax.ShapeDtypeStruct((rows, da), BF16),
        compiler_params=_cparams(("arbitrary",)),
        name="moe_dispatch",
    )(*plan, h2a, pos3)


def _ffn_kernel(x_ref, wg_ref, wu_ref, wd_ref, y_ref):
    e = pl.program_id(0)
    d = wg_ref.shape[0]
    x = x_ref[:, :d]
    gl = x_ref[:, d:].astype(F32)
    lane = lax.broadcasted_iota(I32, gl.shape, 1)
    mine = jnp.logical_or(lane == e, lane == e + N_EXPERTS)
    gate = jnp.sum(jnp.where(mine, gl, 0.0), axis=-1, keepdims=True)
    g = jnp.dot(x, wg_ref[...], preferred_element_type=F32)
    u = jnp.dot(x, wu_ref[...], preferred_element_type=F32)
    hid = (g * jax.nn.sigmoid(g) * u).astype(BF16)
    y_ref[...] = (jnp.dot(hid, wd_ref[...], preferred_element_type=F32) * gate).astype(BF16)


def _ffn_tile(cp_tot):
    for tm in (512, 256, 128, 64):
        if cp_tot % tm == 0:
            return tm
    raise ValueError("slot rows per expert must be a multiple of 64")


def _ffn(xe, wg, wu, wd, cp_tot, e0):
    rows, da = xe.shape
    d, ff = wg.shape[1], wg.shape[2]
    tm = _ffn_tile(cp_tot)
    tiles = cp_tot // tm
    return pl.pallas_call(
        _ffn_kernel,
        grid=(N_EXPERTS, tiles),
        in_specs=[
            pl.BlockSpec((tm, da), lambda e, r: (e * tiles + r, 0)),
            pl.BlockSpec((None, d, ff), lambda e, r: (e + e0, 0, 0), pipeline_mode=pl.Buffered(1)),
            pl.BlockSpec((None, d, ff), lambda e, r: (e + e0, 0, 0), pipeline_mode=pl.Buffered(1)),
            pl.BlockSpec((None, ff, d), lambda e, r: (e + e0, 0, 0), pipeline_mode=pl.Buffered(1)),
        ],
        out_specs=pl.BlockSpec((tm, d), lambda e, r: (e * tiles + r, 0)),
        out_shape=jax.ShapeDtypeStruct((rows, d), BF16),
        compiler_params=_cparams(("arbitrary", "arbitrary")),
        name="expert_ffn",
    )(xe, wg, wu, wd)


def _combine_kernel(a0_ref, nch_ref, ye_ref, pos_ref, x_ref, gt_ref, fg_ref, o_ref,
                    win_ref, xwin_ref, acc_ref, sems, xsems, *, layout, final, first_block):
    i = pl.program_id(0)
    b = i + first_block
    slot = i % 2
    tb, d = x_ref.shape
    dn = (((0,), (0,)), ((), ()))
    a0s = [a0_ref[b * N_EXPERTS + e] for e in range(N_EXPERTS)]
    nchs = [nch_ref[b * N_EXPERTS + e] for e in range(N_EXPERTS)]

    def row_of(blk, e, c):
        return pl.multiple_of(_slot_base(blk, e, layout) + a0_ref[blk * N_EXPERTS + e] + c * SLAB,
                              ROWS_BF16)

    def first_copy(blk, e, buf):
        return pltpu.make_async_copy(ye_ref.at[pl.ds(row_of(blk, e, 0), SLAB)],
                                     win_ref.at[buf, pl.ds(e * SLAB, SLAB)], sems.at[buf, e])

    def extra_copy(e, c):
        return pltpu.make_async_copy(ye_ref.at[pl.ds(row_of(b, e, c), SLAB)], xwin_ref.at[e, c - 1],
                                     xsems.at[e, c - 1])

    @pl.when(i == 0)
    def _():
        for e in range(N_EXPERTS):
            first_copy(b, e, 0).start()

    @pl.when(i + 1 < pl.num_programs(0))
    def _():
        for e in range(N_EXPERTS):
            first_copy(b + 1, e, 1 - slot).start()

    for e in range(N_EXPERTS):
        def fetch(c, carry, e=e):
            extra_copy(e, c).start()
            return carry

        lax.fori_loop(1, nchs[e], fetch, 0)

    pt = _one_hot_stack(pos_ref, a0s, tb)
    for e in range(N_EXPERTS):
        first_copy(b, e, slot).wait()
    acc_ref[...] = lax.dot_general(pt, win_ref[slot], dn, preferred_element_type=F32)

    for e in range(N_EXPERTS):
        def extra(c, carry, e=e):
            extra_copy(e, c).wait()
            ptc = _one_hot_rows(pos_ref, e, a0s[e] + c * SLAB, tb)
            acc_ref[...] += lax.dot_general(ptc, xwin_ref[e, c - 1], dn, preferred_element_type=F32)
            return carry

        lax.fori_loop(1, nchs[e], extra, 0)

    x2 = x_ref[...] + gt_ref[...] * acc_ref[...]
    if final:
        x2 = x2 * lax.rsqrt(jnp.mean(x2 * x2, axis=-1, keepdims=True) + EPS) * fg_ref[...]
    o_ref[...] = x2


def _combine(a0, nch, ye, pos3, x1, mod3, final_g, layout, final, first_block, nblocks):
    d = x1.shape[1]
    tps = SEQ // TB
    fb = first_block
    grid_spec = pltpu.PrefetchScalarGridSpec(
        num_scalar_prefetch=2,
        grid=(nblocks,),
        in_specs=[
            pl.BlockSpec(memory_space=pl.ANY),
            pl.BlockSpec((None, N_EXPERTS, TB), lambda i, *_: (i + fb, 0, 0)),
            pl.BlockSpec((TB, d), lambda i, *_: (i + fb, 0)),
            pl.BlockSpec((None, 1, d), lambda i, *_: (((i + fb) // tps) * N_MOD + 5, 0, 0)),
            pl.BlockSpec((1, d), lambda i, *_: (0, 0)),
        ],
        out_specs=pl.BlockSpec((TB, d), lambda i, *_: (i, 0)),
        scratch_shapes=[
            pltpu.VMEM((2, N_EXPERTS * SLAB, d), BF16),
            pltpu.VMEM((N_EXPERTS, MAXCH - 1, SLAB, d), BF16),
            pltpu.VMEM((TB, d), F32),
            pltpu.SemaphoreType.DMA((2, N_EXPERTS)),
            pltpu.SemaphoreType.DMA((N_EXPERTS, MAXCH - 1)),
        ],
    )
    return pl.pallas_call(
        functools.partial(_combine_kernel, layout=layout, final=final, first_block=fb),
        grid_spec=grid_spec,
        out_shape=jax.ShapeDtypeStruct((nblocks * TB, d), F32),
        compiler_params=_cparams(("arbitrary",)),
        name="moe_combine",
    )(a0, nch, ye, pos3, x1, mod3, final_g)


def _rope_tables():
    half = ROT_DIM // 2
    inv = ROPE_THETA ** (-jnp.arange(0, ROT_DIM, 2, dtype=F32) / ROT_DIM)
    ang = jnp.arange(SEQ, dtype=F32)[:, None] * inv[None, :]
    cos, sin = jnp.cos(ang), jnp.sin(ang)
    dd = jnp.arange(LANES) % HEAD_DIM
    cos_l = jnp.where(dd[None, :] < ROT_DIM, cos[:, dd % half], 1.0)
    sin_l = sin[:, dd % half]
    sa = jnp.where((dd[None, :] >= half) & (dd[None, :] < ROT_DIM), sin_l, 0.0)
    sb = jnp.where(dd[None, :] < half, -sin_l, 0.0)
    return cos_l.astype(F32), sa.astype(F32), sb.astype(F32)


def _trunk(xs, cs, norm1_g, w_ada, b_ada, w_in, b_gate, lam, subln_g, gmlp_ln_g, gmlp_ln_b,
           w_spatial, b_spatial, w_br_attn, w_br_gmlp, w_out, norm2_g, w_router,
           w_e_gate, w_e_up, w_e_down, final_g):
    d = D_MODEL
    assert N_HEADS * 2 * HEAD_DIM == d and GMLP_GROUPS * LANES == d and CHUNK == LANES
    assert SEQ % TB == 0 and TB % CHUNK == 0
    batches = [x.shape[0] for x in xs]
    bt = sum(batches)
    n = bt * SEQ
    n_a = batches[0] * SEQ
    xa, xb, xb_row0 = xs[0].reshape(-1, d), xs[1].reshape(-1, d), 0
    c = jnp.concatenate(cs, axis=0)
    bp = -(-bt // 8) * 8
    c_pad = jnp.pad(c, ((0, bp - bt), (0, 0)))
    mod = _ada(c_pad, w_ada, b_ada)

    caps = [CAPACITY_FACTOR * b * SEQ // N_EXPERTS for b in batches]
    nbs = [b * SEQ // TB for b in batches]
    cps = [cap + PAD for cap in caps]
    cp_tot = sum(cps)
    layout = (nbs[0], cps[0], cp_tot)
    groups = ((0, nbs[0], caps[0]), (nbs[0], nbs[1], caps[1]))
    pad_rows = ((nbs[0] - 1, caps[0]), (nbs[0] + nbs[1] - 1, caps[1]))
    rows = N_EXPERTS * cp_tot
    nblk = n // TB

    cos_t, sa_t, sb_t = _rope_tables()
    ff = w_e_gate.shape[-1]
    w_in_b = _to_bf16(w_in, row_block=d // 4)
    w_eg_b = _to_bf16(w_e_gate.reshape(DEPTH * N_EXPERTS, d, ff))
    w_eu_b = _to_bf16(w_e_up.reshape(DEPTH * N_EXPERTS, d, ff))
    w_ed_b = _to_bf16(w_e_down.reshape(DEPTH * N_EXPERTS, ff, d))
    w_ba_b, w_bg_b, w_o_b = _to_bf16(w_br_attn), _to_bf16(w_br_gmlp), _to_bf16(w_out)
    w_sp_b = _to_bf16(w_spatial.reshape(DEPTH * GMLP_GROUPS, CHUNK, CHUNK))
    outs = None
    for l in range(DEPTH):
        lam_init = 0.8 - 0.6 * math.exp(-0.3 * l)
        mod3 = mod[l, :bt].reshape(bt * N_MOD, 1, d)
        proj = _inproj(xa, xb, n, n_a, xb_row0, mod3, norm1_g[l][None], w_in_b, l,
                       b_gate[l][None], cos_t, sa_t, sb_t)
        attn = _attention(proj.reshape(bt, SEQ, N_SEG * d), lam[l], subln_g[l][None], lam_init)
        wr = jnp.pad(jnp.concatenate([w_router[l], w_router[l]], axis=1),
                     ((0, 0), (0, LANES - 2 * N_EXPERTS)))
        wrh = wr.astype(BF16)
        wrl = (wr - wrh.astype(F32)).astype(BF16)
        x1, h2a, aff = _mix(proj, attn.reshape(n, d), xa, xb, n_a, xb_row0, mod3,
                            gmlp_ln_g[l][None], gmlp_ln_b[l][None],
                            w_sp_b[l * GMLP_GROUPS:(l + 1) * GMLP_GROUPS], b_spatial[l].T,
                            w_ba_b[l], w_bg_b[l], w_o_b[l], norm2_g[l][None], wrh, wrl)
        bits3 = lax.bitcast_convert_type(aff, I32).reshape(nblk, TB, N_EXPERTS).transpose(0, 2, 1)
        pos3, s0_t, cnt_t = _route(bits3, groups)
        plan = _window_plan(s0_t[:, :nblk].T.reshape(-1), cnt_t[:, :nblk].T.reshape(-1))
        xe = _dispatch(plan, h2a, pos3, layout, pad_rows, rows)
        ye = _ffn(xe, w_eg_b, w_eu_b, w_ed_b, cp_tot, l * N_EXPERTS)
        if l < DEPTH - 1:
            x = _combine(plan[0], plan[2], ye, pos3, x1, mod3, final_g[None], layout, False,
                         0, nblk)
            xa, xb, xb_row0 = x, x, n_a
        else:
            outs = tuple(
                _combine(plan[0], plan[2], ye, pos3, x1, mod3, final_g[None], layout, True,
                         fb, nb).reshape(bsz, SEQ, d)
                for fb, nb, bsz in ((0, nbs[0], batches[0]), (nbs[0], nbs[1], batches[1])))
    return outs


def kernel(x_prompt, x_sample, c_prompt, c_sample, norm1_g, w_ada, b_ada, w_in, b_gate, lam, subln_g, gmlp_ln_g, gmlp_ln_b, w_spatial, b_spatial, w_br_attn, w_br_gmlp, w_out, norm2_g, w_router, w_e_gate, w_e_up, w_e_down, final_g):
    return _trunk((x_prompt, x_sample), (c_prompt, c_sample), norm1_g, w_ada, b_ada, w_in,
                  b_gate, lam, subln_g, gmlp_ln_g, gmlp_ln_b, w_spatial, b_spatial, w_br_attn,
                  w_br_gmlp, w_out, norm2_g, w_router, w_e_gate, w_e_up, w_e_down, final_g)
```

```python
import functools
import math

import jax
import jax.numpy as jnp
from jax import lax
from jax.experimental import pallas as pl
from jax.experimental.pallas import tpu as pltpu

F32 = jnp.float32
BF16 = jnp.bfloat16
I32 = jnp.int32

D_MODEL = 1024
SEQ = 4096
DEPTH = 2
N_HEADS = 8
HEAD_DIM = 64
ROT_DIM = HEAD_DIM // 4
ROPE_THETA = 500000.0
CHUNK = 128
GMLP_GROUPS = 8
N_EXPERTS = 16
EXPERT_FF = 2048
CAPACITY_FACTOR = 2
N_MOD = 6
N_SEG = 7
EPS = 1e-6

LANES = 128
VMEM_LIMIT = 56 * 1024 * 1024

TM_PROJ = 512
NC_PROJ = 256
TQ = 256
NSUB_Q = 4
TK = 1024
SCORE_BOUND = 60.0
TM_MIX = 256
TB = 256
SLAB = 64
ROWS_BF16 = 16
MAXCH = -(-(TB + ROWS_BF16 - 1) // SLAB)
PAD = TB


def _cparams(sem):
    return pltpu.CompilerParams(dimension_semantics=sem, vmem_limit_bytes=VMEM_LIMIT)


def _ada_kernel(c_ref, w_ref, b_ref, o_ref):
    c = c_ref[...]
    a = c * jax.nn.sigmoid(c)
    o_ref[0] = jnp.dot(a, w_ref[0], preferred_element_type=F32,
                       precision=lax.Precision.HIGHEST) + b_ref[0]


def _ada(c_pad, w_ada, b_ada):
    bp, d = c_pad.shape
    depth = w_ada.shape[0]
    return pl.pallas_call(
        _ada_kernel,
        grid=(depth, N_MOD),
        in_specs=[
            pl.BlockSpec((bp, d), lambda l, j: (0, 0)),
            pl.BlockSpec((1, d, d), lambda l, j: (l, 0, j)),
            pl.BlockSpec((1, 1, d), lambda l, j: (l, 0, j)),
        ],
        out_specs=pl.BlockSpec((1, bp, d), lambda l, j: (l, 0, j)),
        out_shape=jax.ShapeDtypeStruct((depth, bp, N_MOD * d), F32),
        compiler_params=_cparams(("arbitrary", "arbitrary")),
        name="ada_mod",
    )(c_pad, w_ada, b_ada.reshape(depth, 1, N_MOD * d))


def _gelu(x):
    return jax.nn.gelu(x)


def _cast_kernel(x_ref, o_ref):
    o_ref[...] = x_ref[...].astype(o_ref.dtype)


def _to_bf16(w, row_block=None):
    lead, r, c = w.shape
    rb = r if row_block is None else row_block
    spec = pl.BlockSpec((None, rb, c), lambda i, k: (i, k, 0))
    return pl.pallas_call(
        _cast_kernel,
        grid=(lead, r // rb),
        in_specs=[spec],
        out_specs=spec,
        out_shape=jax.ShapeDtypeStruct(w.shape, BF16),
        compiler_params=_cparams(("arbitrary", "arbitrary")),
        name="to_bf16",
    )(w)


def _inproj_kernel(xa_ref, xb_ref, sh_ref, sc_ref, g_ref, w_ref, bg_ref, cos_ref, sa_ref, sb_ref,
                   o_ref, *, tiles_a):
    x = jnp.where(pl.program_id(0) < tiles_a, xa_ref[...], xb_ref[...])
    r = x * lax.rsqrt(jnp.mean(x * x, axis=-1, keepdims=True) + EPS) * g_ref[...]
    h = (r * (1.0 + sc_ref[...]) + sh_ref[...]).astype(BF16)
    d = h.shape[1]
    half = ROT_DIM // 2
    q_scale = HEAD_DIM ** -0.5 * math.log2(math.e)
    rope_q = (cos_ref[...] * q_scale, sa_ref[...] * q_scale, sb_ref[...] * q_scale)
    rope_k = (cos_ref[...], sa_ref[...], sb_ref[...])

    def rope(acc, tables):
        cos, sa, sb = tables
        parts = []
        for c in range(acc.shape[1] // LANES):
            a = acc[:, c * LANES:(c + 1) * LANES]
            parts.append(a * cos + pltpu.roll(a, half, 1) * sa
                         + pltpu.roll(a, LANES - half, 1) * sb)
        return jnp.concatenate(parts, axis=-1)

    for c in range(w_ref.shape[1] // NC_PROJ):
        cols = slice(c * NC_PROJ, (c + 1) * NC_PROJ)
        seg = c * NC_PROJ // d
        acc = jnp.dot(h, w_ref[:, cols], preferred_element_type=F32)
        if seg == 0:
            out = rope(acc, rope_q)
        elif seg == 1:
            out = rope(acc, rope_k)
        elif seg == 2:
            out = acc
        elif seg in (3, 4):
            out = _gelu(acc)
        else:
            out = jax.nn.sigmoid(acc + bg_ref[:, c * NC_PROJ - 5 * d:(c + 1) * NC_PROJ - 5 * d])
        o_ref[:, cols] = out.astype(BF16)


def _two_group_specs(rows, d, tm, rows_a, xb_row0):
    ta, tb0 = rows_a // tm, xb_row0 // tm
    return ta, [pl.BlockSpec((tm, d), lambda i, *_: (jnp.minimum(i, ta - 1), 0)),
                pl.BlockSpec((tm, d), lambda i, *_: (jnp.maximum(i - ta, 0) + tb0, 0))]


def _inproj(xa, xb, n, rows_a, xb_row0, mod3, norm_g, w_in, layer, b_gate, cos_t, sa_t, sb_t):
    d = xa.shape[1]
    tm = min(TM_PROJ, SEQ)
    tiles_per_seq = SEQ // tm
    tiles_a, xspecs = _two_group_specs(n, d, tm, rows_a, xb_row0)
    return pl.pallas_call(
        functools.partial(_inproj_kernel, tiles_a=tiles_a),
        grid=(n // tm,),
        in_specs=xspecs + [
            pl.BlockSpec((None, 1, d), lambda i: ((i // tiles_per_seq) * N_MOD + 0, 0, 0)),
            pl.BlockSpec((None, 1, d), lambda i: ((i // tiles_per_seq) * N_MOD + 1, 0, 0)),
            pl.BlockSpec((1, d), lambda i: (0, 0)),
            pl.BlockSpec((None, d, N_SEG * d), lambda i: (layer, 0, 0),
                         pipeline_mode=pl.Buffered(1)),
            pl.BlockSpec((1, 2 * d), lambda i: (0, 0)),
            pl.BlockSpec((tm, LANES), lambda i: (i % tiles_per_seq, 0)),
            pl.BlockSpec((tm, LANES), lambda i: (i % tiles_per_seq, 0)),
            pl.BlockSpec((tm, LANES), lambda i: (i % tiles_per_seq, 0)),
        ],
        out_specs=pl.BlockSpec((tm, N_SEG * d), lambda i: (i, 0)),
        out_shape=jax.ShapeDtypeStruct((n, N_SEG * d), BF16),
        compiler_params=_cparams(("arbitrary",)),
        name="in_proj",
    )(xa, xb, mod3, mod3, norm_g, w_in, b_gate, cos_t, sa_t, sb_t)


def _attn_kernel(q_ref, k_ref, v_ref, lam_ref, sg_ref, o_ref, vaug_ref, p_ref, kn_ref, *,
                 lam_init):
    tqt, vd = q_ref.shape
    s_len = k_ref.shape[0]
    tk = min(TK, s_len)
    tq = min(TQ, tqt)
    nsub = tqt // tq
    nch = s_len // tk

    @pl.when(pl.program_id(2) == 0)
    def _():
        vaug_ref[:, :vd] = v_ref[...]
        vaug_ref[:, vd:] = jnp.ones((s_len, vd), BF16)
        kf = k_ref[...].astype(F32)
        kn_ref[0] = jnp.max(jnp.sum(kf * kf, axis=-1, keepdims=True))

    qf = q_ref[...].astype(F32)
    qn = jnp.max(jnp.sum(qf * qf, axis=-1, keepdims=True))
    dn = (((1,), (1,)), ((), ()))
    lp = lam_ref[...]
    lam = (jnp.exp(jnp.sum(lp[0:1] * lp[1:2], axis=-1, keepdims=True))
           - jnp.exp(jnp.sum(lp[2:3] * lp[3:4], axis=-1, keepdims=True)) + lam_init)

    def stacked(q):
        lane = lax.broadcasted_iota(I32, q.shape, 1)
        zero = jnp.zeros_like(q)
        return jnp.concatenate([jnp.where(lane < HEAD_DIM, q, zero),
                                jnp.where(lane >= HEAD_DIM, q, zero)], axis=0)

    def sub_ln(o):
        r = o * lax.rsqrt(jnp.mean(o * o, axis=-1, keepdims=True) + EPS)
        return ((r * sg_ref[...]) * (1.0 - lam_init)).astype(BF16)

    small = qn * kn_ref[0] <= SCORE_BOUND * SCORE_BOUND

    @pl.when(small)
    def _():
        prev = None
        for t in range(nsub + 1):
            if t < nsub:
                qq = stacked(q_ref[t * tq:(t + 1) * tq, :])
                l = jnp.zeros((2 * tq, 1), F32)
            if prev is not None:
                pt, pl_sum = prev
                a1 = lam * pl_sum[:tq] / pl_sum[tq:]
                acc = jnp.zeros((tq, vd), F32)
            for c in range(nch):
                ks = slice(c * tk, (c + 1) * tk)
                if t < nsub:
                    p = jnp.exp2(lax.dot_general(qq, k_ref[ks, :], dn, preferred_element_type=F32))
                    p_ref[t % 2, :, ks] = p
                    l = l + jnp.sum(p, axis=-1, keepdims=True)
                if prev is not None:
                    pd = p_ref[pt % 2, :tq, ks] - p_ref[pt % 2, tq:, ks] * a1
                    acc = acc + jnp.dot(pd.astype(BF16), v_ref[ks, :],
                                        preferred_element_type=F32)
            if prev is not None:
                o_ref[pt * tq:(pt + 1) * tq, :] = sub_ln(acc / pl_sum[:tq])
            prev = (t, l) if t < nsub else None

    @pl.when(jnp.logical_not(small))
    def _():
        def tile(t, carry):
            rows = pl.ds(pl.multiple_of(t * tq, tq), tq)
            qq = stacked(q_ref[rows, :])
            m = jnp.full((2 * tq, 1), -jnp.inf, F32)
            acc = jnp.zeros((2 * tq, 2 * vd), F32)
            for c in range(nch):
                ks = slice(c * tk, (c + 1) * tk)
                s = lax.dot_general(qq, k_ref[ks, :], dn, preferred_element_type=F32)
                m_new = jnp.maximum(m, jnp.max(s, axis=-1, keepdims=True))
                alpha = jnp.exp2(m - m_new)
                p = jnp.exp2(s - m_new).astype(BF16)
                acc = alpha * acc + jnp.dot(p, vaug_ref[ks, :], preferred_element_type=F32)
                m = m_new
            on = acc[:, :vd] / acc[:, vd:]
            o_ref[rows, :] = sub_ln(on[:tq] - lam * on[tq:])
            return carry

        lax.fori_loop(0, nsub, tile, 0)


def _attention(proj3, lam, subln_g, lam_init):
    bt, s, _ = proj3.shape
    vd = 2 * HEAD_DIM
    tq = min(TQ, s)
    tqt = min(TQ * NSUB_Q, s)
    return pl.pallas_call(
        functools.partial(_attn_kernel, lam_init=lam_init),
        grid=(bt, N_HEADS, s // tqt),
        in_specs=[
            pl.BlockSpec((None, tqt, vd), lambda b, h, i: (b, i, h)),
            pl.BlockSpec((None, s, vd), lambda b, h, i: (b, 0, N_HEADS + h)),
            pl.BlockSpec((None, s, vd), lambda b, h, i: (b, 0, 2 * N_HEADS + h)),
            pl.BlockSpec((4, HEAD_DIM), lambda b, h, i: (0, 0)),
            pl.BlockSpec((1, vd), lambda b, h, i: (0, 0)),
        ],
        out_specs=pl.BlockSpec((None, tqt, vd), lambda b, h, i: (b, i, h)),
        out_shape=jax.ShapeDtypeStruct((bt, s, N_HEADS * vd), BF16),
        scratch_shapes=[pltpu.VMEM((s, 2 * vd), BF16), pltpu.VMEM((2, 2 * tq, s), F32),
                        pltpu.SMEM((1,), F32)],
        compiler_params=_cparams(("arbitrary", "arbitrary", "arbitrary")),
        name="diff_attn",
    )(proj3, proj3, proj3, lam, subln_g)


def _mix_kernel(u_ref, vg_ref, ga_ref, gg_ref, at_ref, xa_ref, xb_ref, lng_ref, lnb_ref, ws_ref,
                bs_ref, wba_ref, wbg_ref, wo_ref, gt1_ref, n2g_ref, sh2_ref, sc2_ref, wrh_ref,
                wrl_ref, x1_ref, h2_ref, aff_ref, *, tiles_a):
    tm = u_ref.shape[0]
    vg = vg_ref[...].astype(F32)
    mu = jnp.mean(vg, axis=-1, keepdims=True)
    xc = vg - mu
    var = jnp.mean(xc * xc, axis=-1, keepdims=True)
    vn = (xc * lax.rsqrt(var + EPS) * lng_ref[...] + lnb_ref[...]).astype(BF16)
    bs = bs_ref[...]
    cols = []
    for g in range(GMLP_GROUPS):
        rows = []
        for c in range(tm // CHUNK):
            blk = vn[c * CHUNK:(c + 1) * CHUNK, g * LANES:(g + 1) * LANES]
            rows.append(jnp.dot(ws_ref[g], blk, preferred_element_type=F32) + bs[:, g:g + 1])
        cols.append(jnp.concatenate(rows, axis=0))
    mixed = jnp.concatenate(cols, axis=1)
    gm = (u_ref[...].astype(F32) * mixed).astype(BF16)
    o_g = jnp.dot(gm, wbg_ref[...], preferred_element_type=F32)
    o_a = jnp.dot(at_ref[...], wba_ref[...], preferred_element_type=F32)
    merged = (ga_ref[...].astype(F32) * o_a + gg_ref[...].astype(F32) * o_g).astype(BF16)
    mo = jnp.dot(merged, wo_ref[...], preferred_element_type=F32)
    x_in = jnp.where(pl.program_id(0) < tiles_a, xa_ref[...], xb_ref[...])
    x1 = x_in + gt1_ref[...] * mo
    x1_ref[...] = x1
    r = x1 * lax.rsqrt(jnp.mean(x1 * x1, axis=-1, keepdims=True) + EPS) * n2g_ref[...]
    h2 = r * (1.0 + sc2_ref[...]) + sh2_ref[...]
    h2h = h2.astype(BF16)
    d = h2.shape[1]
    h2_ref[:, :d] = h2h
    h2l = (h2 - h2h.astype(F32)).astype(BF16)
    wrh = wrh_ref[...]
    logits = (jnp.dot(h2h, wrh, preferred_element_type=F32)
              + jnp.dot(h2l, wrh, preferred_element_type=F32)
              + jnp.dot(h2h, wrl_ref[...], preferred_element_type=F32))
    lane = lax.broadcasted_iota(I32, logits.shape, 1)
    first = lane < N_EXPERTS
    m = jnp.max(jnp.where(first, logits, -jnp.inf), axis=-1, keepdims=True)
    p = jnp.exp(logits - m)
    aff = p / jnp.sum(jnp.where(first, p, 0.0), axis=-1, keepdims=True)
    aff_ref[...] = aff[:, :N_EXPERTS]
    hi = aff.astype(BF16)
    lo = (aff - hi.astype(F32)).astype(BF16)
    h2_ref[:, d:] = jnp.where(first, hi, jnp.where(lane < 2 * N_EXPERTS, lo, jnp.zeros_like(lo)))


def _mix(proj, attn, xa, xb, rows_a, xb_row0, mod3, lng, lnb, ws, bs_t, wba, wbg, wo, n2g, wrh,
         wrl):
    n, d = attn.shape
    tm = min(TM_MIX, SEQ)
    tps = SEQ // tm
    tiles_a, xspecs = _two_group_specs(n, d, tm, rows_a, xb_row0)
    row = lambda i: (i, 0)
    full2 = lambda i: (0, 0)

    def seg(k):
        return pl.BlockSpec((tm, d), lambda i: (i, k))

    def modspec(k):
        return pl.BlockSpec((None, 1, d), lambda i: ((i // tps) * N_MOD + k, 0, 0))

    return pl.pallas_call(
        functools.partial(_mix_kernel, tiles_a=tiles_a),
        grid=(n // tm,),
        in_specs=[
            seg(3), seg(4), seg(5), seg(6),
            pl.BlockSpec((tm, d), row),
        ] + xspecs + [
            pl.BlockSpec((1, d), full2),
            pl.BlockSpec((1, d), full2),
            pl.BlockSpec((GMLP_GROUPS, CHUNK, CHUNK), lambda i: (0, 0, 0)),
            pl.BlockSpec((CHUNK, GMLP_GROUPS), full2),
            pl.BlockSpec((d, d), full2),
            pl.BlockSpec((d, d), full2),
            pl.BlockSpec((d, d), full2),
            modspec(2),
            pl.BlockSpec((1, d), full2),
            modspec(3),
            modspec(4),
            pl.BlockSpec((d, LANES), full2),
            pl.BlockSpec((d, LANES), full2),
        ],
        out_specs=[
            pl.BlockSpec((tm, d), row),
            pl.BlockSpec((tm, d + LANES), row),
            pl.BlockSpec((tm, N_EXPERTS), row),
        ],
        out_shape=[
            jax.ShapeDtypeStruct((n, d), F32),
            jax.ShapeDtypeStruct((n, d + LANES), BF16),
            jax.ShapeDtypeStruct((n, N_EXPERTS), F32),
        ],
        compiler_params=_cparams(("arbitrary",)),
        name="branch_mix",
    )(proj, proj, proj, proj, attn, xa, xb, lng, lnb, ws, bs_t, wba, wbg, wo, mod3, n2g, mod3,
      mod3, wrh, wrl)


def _route_kernel(bits_ref, pos_ref, s0_ref, cnt_ref, *, groups):
    ne, tb = bits_ref.shape[1], bits_ref.shape[2]
    ri = lax.broadcasted_iota(I32, (tb, tb), 0)
    ci = lax.broadcasted_iota(I32, (tb, tb), 1)
    tri = jnp.where(ri < ci, 1.0, 0.0).astype(BF16)
    blk_lane = lax.broadcasted_iota(I32, s0_ref.shape, 1)
    s0_ref[...] = jnp.zeros(s0_ref.shape, I32)
    cnt_ref[...] = jnp.zeros(cnt_ref.shape, I32)

    for (b0, nb, cap) in groups:
        def count(pred_fn):
            def body(i, acc):
                return acc + jnp.where(pred_fn(bits_ref[b0 + i]), 1, 0)
            acc = lax.fori_loop(0, nb, body, jnp.zeros((ne, tb), I32), unroll=8)
            return jnp.sum(acc, axis=1, keepdims=True)

        def bs_body(k, thr):
            cand = thr | lax.shift_left(jnp.int32(1), jnp.int32(30) - k)
            c = count(lambda blk: blk >= cand)
            return jnp.where(c >= cap, cand, thr)

        thr = lax.fori_loop(0, 31, bs_body, jnp.zeros((ne, 1), I32))
        n_gt = count(lambda blk: blk > thr)
        need = (cap - n_gt).astype(F32)

        def scan_body(i, carry):
            ceq, cpos = carry
            blk = bits_ref[b0 + i]
            gt = blk > thr
            eq = blk == thr
            eqf = jnp.where(eq, 1.0, 0.0)
            eq_excl = jnp.dot(eqf.astype(BF16), tri, preferred_element_type=F32) + ceq
            self = jnp.where(gt, 1.0, jnp.where(eq_excl < need, eqf, 0.0))
            pos_excl = jnp.dot(self.astype(BF16), tri, preferred_element_type=F32) + cpos
            pos_ref[b0 + i] = jnp.where(self > 0.5, pos_excl.astype(I32), -1)
            n_sel = jnp.sum(self, axis=1, keepdims=True)
            hit = blk_lane == (b0 + i)
            s0_ref[...] = jnp.where(hit, cpos.astype(I32), s0_ref[...])
            cnt_ref[...] = jnp.where(hit, n_sel.astype(I32), cnt_ref[...])
            return (ceq + jnp.sum(eqf, axis=1, keepdims=True), cpos + n_sel)

        zero = jnp.zeros((ne, 1), F32)
        lax.fori_loop(0, nb, scan_body, (zero, zero))


def _route(bits3, groups):
    nblk, ne, tb = bits3.shape
    nbp = -(-nblk // LANES) * LANES
    return pl.pallas_call(
        functools.partial(_route_kernel, groups=groups),
        out_shape=[
            jax.ShapeDtypeStruct((nblk, ne, tb), I32),
            jax.ShapeDtypeStruct((ne, nbp), I32),
            jax.ShapeDtypeStruct((ne, nbp), I32),
        ],
        compiler_params=pltpu.CompilerParams(vmem_limit_bytes=VMEM_LIMIT),
        name="route_topc",
    )(bits3)


def _slot_base(b, e, layout):
    nb0, cp0, cp_tot = layout
    return e * cp_tot + jnp.where(b >= nb0, cp0, 0)


def _one_hot_rows(pos_ref, e, first_slot, tb):
    jrow = lax.broadcasted_iota(I32, (SLAB, tb), 0)
    return jnp.where(jrow == pos_ref[e:e + 1, :] - first_slot, 1.0, 0.0).astype(BF16)


def _one_hot_stack(pos_ref, first_slots, tb):
    return jnp.concatenate([_one_hot_rows(pos_ref, e, first_slots[e], tb)
                            for e in range(N_EXPERTS)], axis=0)


def _window_plan(s0, cnt):
    a0 = (s0 // ROWS_BF16) * ROWS_BF16
    r = s0 - a0
    nch = jnp.where(cnt > 0, (r + cnt + SLAB - 1) // SLAB, 0)
    nxt = ((s0 + cnt) // ROWS_BF16) * ROWS_BF16 - a0
    c1 = jnp.minimum(nxt // SLAB, jnp.maximum(nch - 1, 0))
    o1 = jnp.where(nxt // SLAB == c1, nxt % SLAB, SLAB - ROWS_BF16)
    return a0, r, nch, c1, o1


def _dispatch_kernel(a0_ref, r_ref, nch_ref, c1_ref, o1_ref, h_ref, pos_ref, xe_ref,
                     slab_ref, carry_ref, zero_ref, sems, *, layout, pad_rows):
    b = pl.program_id(0)
    tb, da = h_ref.shape

    @pl.when(b == 0)
    def _():
        carry_ref[...] = jnp.zeros(carry_ref.shape, BF16)

    h = h_ref[...]
    a0s = [a0_ref[b * N_EXPERTS + e] for e in range(N_EXPERTS)]
    nchs = [nch_ref[b * N_EXPERTS + e] for e in range(N_EXPERTS)]
    y0 = jnp.dot(_one_hot_stack(pos_ref, a0s, tb), h, preferred_element_type=F32).astype(BF16)
    row16 = lax.broadcasted_iota(I32, (ROWS_BF16, da), 0)

    def copy(e, c, blk=b):
        row = pl.multiple_of(_slot_base(blk, e, layout) + a0_ref[blk * N_EXPERTS + e] + c * SLAB,
                             ROWS_BF16)
        return pltpu.make_async_copy(slab_ref.at[e, c], xe_ref.at[pl.ds(row, SLAB)], sems.at[e, c])

    def drain_block(blk):
        for e in range(N_EXPERTS):
            def drain(c, carry, e=e):
                copy(e, c, blk).wait()
                return carry

            lax.fori_loop(0, nch_ref[blk * N_EXPERTS + e], drain, 0)

    prev_pending = b > 0
    for (last_b, _) in pad_rows:
        prev_pending = jnp.logical_and(prev_pending, b - 1 != last_b)

    @pl.when(prev_pending)
    def _():
        drain_block(b - 1)

    for e in range(N_EXPERTS):
        @pl.when(nchs[e] > 0)
        def _(e=e):
            ye = y0[e * SLAB:(e + 1) * SLAB]
            slab_ref[e, 0, :ROWS_BF16, :] = jnp.where(row16 < r_ref[b * N_EXPERTS + e],
                                                      carry_ref[e], ye[:ROWS_BF16])
            slab_ref[e, 0, ROWS_BF16:, :] = ye[ROWS_BF16:]
            copy(e, 0).start()

            def extra(c, carry):
                pt = _one_hot_rows(pos_ref, e, a0s[e] + c * SLAB, tb)
                slab_ref[e, c] = jnp.dot(pt, h, preferred_element_type=F32).astype(BF16)
                copy(e, c).start()
                return carry

            lax.fori_loop(1, nchs[e], extra, 0)
            o1 = pl.multiple_of(o1_ref[b * N_EXPERTS + e], ROWS_BF16)
            carry_ref[e] = slab_ref[e, c1_ref[b * N_EXPERTS + e], pl.ds(o1, ROWS_BF16), :]

    for (last_b, first_pad) in pad_rows:
        @pl.when(b == last_b)
        def _(last_b=last_b, first_pad=first_pad):
            drain_block(last_b)
            zero_ref[...] = jnp.zeros(zero_ref.shape, BF16)
            cps = []
            for e in range(N_EXPERTS):
                for c in range(PAD // SLAB):
                    r0 = _slot_base(last_b, e, layout) + first_pad + c * SLAB
                    cp = pltpu.make_async_copy(zero_ref, xe_ref.at[pl.ds(r0, SLAB)], sems.at[e, c])
                    cp.start()
                    cps.append(cp)
            for cp in cps:
                cp.wait()


def _dispatch(plan, h2a, pos3, layout, pad_rows, rows):
    n, da = h2a.shape
    grid_spec = pltpu.PrefetchScalarGridSpec(
        num_scalar_prefetch=5,
        grid=(n // TB,),
        in_specs=[
            pl.BlockSpec((TB, da), lambda b, *_: (b, 0)),
            pl.BlockSpec((None, N_EXPERTS, TB), lambda b, *_: (b, 0, 0)),
        ],
        out_specs=pl.BlockSpec(memory_space=pl.ANY),
        scratch_shapes=[
            pltpu.VMEM((N_EXPERTS, MAXCH, SLAB, da), BF16),
            pltpu.VMEM((N_EXPERTS, ROWS_BF16, da), BF16),
            pltpu.VMEM((SLAB, da), BF16),
            pltpu.SemaphoreType.DMA((N_EXPERTS, MAXCH)),
        ],
    )
    return pl.pallas_call(
        functools.partial(_dispatch_kernel, layout=layout, pad_rows=pad_rows),
        grid_spec=grid_spec,
        out_shape=jax.ShapeDtypeStruct((rows, da), BF16),
        compiler_params=_cparams(("arbitrary",)),
        name="moe_dispatch",
    )(*plan, h2a, pos3)


def _ffn_kernel(x_ref, wg_ref, wu_ref, wd_ref, y_ref):
    e = pl.program_id(0)
    d = wg_ref.shape[0]
    x = x_ref[:, :d]
    gl = x_ref[:, d:].astype(F32)
    lane = lax.broadcasted_iota(I32, gl.shape, 1)
    mine = jnp.logical_or(lane == e, lane == e + N_EXPERTS)
    gate = jnp.sum(jnp.where(mine, gl, 0.0), axis=-1, keepdims=True)
    g = jnp.dot(x, wg_ref[...], preferred_element_type=F32)
    u = jnp.dot(x, wu_ref[...], preferred_element_type=F32)
    hid = (g * jax.nn.sigmoid(g) * u).astype(BF16)
    y_ref[...] = (jnp.dot(hid, wd_ref[...], preferred_element_type=F32) * gate).astype(BF16)


def _ffn_tile(cp_tot):
    for tm in (512, 256, 128, 64):
        if cp_tot % tm == 0:
            return tm
    raise ValueError("slot rows per expert must be a multiple of 64")


def _ffn(xe, wg, wu, wd, cp_tot, e0):
    rows, da = xe.shape
    d, ff = wg.shape[1], wg.shape[2]
    tm = _ffn_tile(cp_tot)
    tiles = cp_tot // tm
    return pl.pallas_call(
        _ffn_kernel,
        grid=(N_EXPERTS, tiles),
        in_specs=[
            pl.BlockSpec((tm, da), lambda e, r: (e * tiles + r, 0)),
            pl.BlockSpec((None, d, ff), lambda e, r: (e + e0, 0, 0)),
            pl.BlockSpec((None, d, ff), lambda e, r: (e + e0, 0, 0)),
            pl.BlockSpec((None, ff, d), lambda e, r: (e + e0, 0, 0)),
        ],
        out_specs=pl.BlockSpec((tm, d), lambda e, r: (e * tiles + r, 0)),
        out_shape=jax.ShapeDtypeStruct((rows, d), BF16),
        compiler_params=_cparams(("arbitrary", "arbitrary")),
        name="expert_ffn",
    )(xe, wg, wu, wd)


def _combine_kernel(a0_ref, nch_ref, ye_ref, pos_ref, x_ref, gt_ref, fg_ref, o_ref,
                    win_ref, xwin_ref, acc_ref, sems, xsems, *, layout, final, first_block):
    i = pl.program_id(0)
    b = i + first_block
    slot = i % 2
    tb, d = x_ref.shape
    dn = (((0,), (0,)), ((), ()))
    a0s = [a0_ref[b * N_EXPERTS + e] for e in range(N_EXPERTS)]
    nchs = [nch_ref[b * N_EXPERTS + e] for e in range(N_EXPERTS)]

    def row_of(blk, e, c):
        return pl.multiple_of(_slot_base(blk, e, layout) + a0_ref[blk * N_EXPERTS + e] + c * SLAB,
                              ROWS_BF16)

    def first_copy(blk, e, buf):
        return pltpu.make_async_copy(ye_ref.at[pl.ds(row_of(blk, e, 0), SLAB)],
                                     win_ref.at[buf, pl.ds(e * SLAB, SLAB)], sems.at[buf, e])

    def extra_copy(e, c):
        return pltpu.make_async_copy(ye_ref.at[pl.ds(row_of(b, e, c), SLAB)], xwin_ref.at[e, c - 1],
                                     xsems.at[e, c - 1])

    @pl.when(i == 0)
    def _():
        for e in range(N_EXPERTS):
            first_copy(b, e, 0).start()

    @pl.when(i + 1 < pl.num_programs(0))
    def _():
        for e in range(N_EXPERTS):
            first_copy(b + 1, e, 1 - slot).start()

    for e in range(N_EXPERTS):
        def fetch(c, carry, e=e):
            extra_copy(e, c).start()
            return carry

        lax.fori_loop(1, nchs[e], fetch, 0)

    pt = _one_hot_stack(pos_ref, a0s, tb)
    for e in range(N_EXPERTS):
        first_copy(b, e, slot).wait()
    acc_ref[...] = lax.dot_general(pt, win_ref[slot], dn, preferred_element_type=F32)

    for e in range(N_EXPERTS):
        def extra(c, carry, e=e):
            extra_copy(e, c).wait()
            ptc = _one_hot_rows(pos_ref, e, a0s[e] + c * SLAB, tb)
            acc_ref[...] += lax.dot_general(ptc, xwin_ref[e, c - 1], dn, preferred_element_type=F32)
            return carry

        lax.fori_loop(1, nchs[e], extra, 0)

    x2 = x_ref[...] + gt_ref[...] * acc_ref[...]
    if final:
        x2 = x2 * lax.rsqrt(jnp.mean(x2 * x2, axis=-1, keepdims=True) + EPS) * fg_ref[...]
    o_ref[...] = x2


def _combine(a0, nch, ye, pos3, x1, mod3, final_g, layout, final, first_block, nblocks):
    d = x1.shape[1]
    tps = SEQ // TB
    fb = first_block
    grid_spec = pltpu.PrefetchScalarGridSpec(
        num_scalar_prefetch=2,
        grid=(nblocks,),
        in_specs=[
            pl.BlockSpec(memory_space=pl.ANY),
            pl.BlockSpec((None, N_EXPERTS, TB), lambda i, *_: (i + fb, 0, 0)),
            pl.BlockSpec((TB, d), lambda i, *_: (i + fb, 0)),
            pl.BlockSpec((None, 1, d), lambda i, *_: (((i + fb) // tps) * N_MOD + 5, 0, 0)),
            pl.BlockSpec((1, d), lambda i, *_: (0, 0)),
        ],
        out_specs=pl.BlockSpec((TB, d), lambda i, *_: (i, 0)),
        scratch_shapes=[
            pltpu.VMEM((2, N_EXPERTS * SLAB, d), BF16),
            pltpu.VMEM((N_EXPERTS, MAXCH - 1, SLAB, d), BF16),
            pltpu.VMEM((TB, d), F32),
            pltpu.SemaphoreType.DMA((2, N_EXPERTS)),
            pltpu.SemaphoreType.DMA((N_EXPERTS, MAXCH - 1)),
        ],
    )
    return pl.pallas_call(
        functools.partial(_combine_kernel, layout=layout, final=final, first_block=fb),
        grid_spec=grid_spec,
        out_shape=jax.ShapeDtypeStruct((nblocks * TB, d), F32),
        compiler_params=_cparams(("arbitrary",)),
        name="moe_combine",
    )(a0, nch, ye, pos3, x1, mod3, final_g)


def _rope_tables():
    half = ROT_DIM // 2
    inv = ROPE_THETA ** (-jnp.arange(0, ROT_DIM, 2, dtype=F32) / ROT_DIM)
    ang = jnp.arange(SEQ, dtype=F32)[:, None] * inv[None, :]
    cos, sin = jnp.cos(ang), jnp.sin(ang)
    dd = jnp.arange(LANES) % HEAD_DIM
    cos_l = jnp.where(dd[None, :] < ROT_DIM, cos[:, dd % half], 1.0)
    sin_l = sin[:, dd % half]
    sa = jnp.where((dd[None, :] >= half) & (dd[None, :] < ROT_DIM), sin_l, 0.0)
    sb = jnp.where(dd[None, :] < half, -sin_l, 0.0)
    return cos_l.astype(F32), sa.astype(F32), sb.astype(F32)


def _trunk(xs, cs, norm1_g, w_ada, b_ada, w_in, b_gate, lam, subln_g, gmlp_ln_g, gmlp_ln_b,
           w_spatial, b_spatial, w_br_attn, w_br_gmlp, w_out, norm2_g, w_router,
           w_e_gate, w_e_up, w_e_down, final_g):
    d = D_MODEL
    assert N_HEADS * 2 * HEAD_DIM == d and GMLP_GROUPS * LANES == d and CHUNK == LANES
    assert SEQ % TB == 0 and TB % CHUNK == 0
    batches = [x.shape[0] for x in xs]
    bt = sum(batches)
    n = bt * SEQ
    n_a = batches[0] * SEQ
    xa, xb, xb_row0 = xs[0].reshape(-1, d), xs[1].reshape(-1, d), 0
    c = jnp.concatenate(cs, axis=0)
    bp = -(-bt // 8) * 8
    c_pad = jnp.pad(c, ((0, bp - bt), (0, 0)))
    mod = _ada(c_pad, w_ada, b_ada)

    caps = [CAPACITY_FACTOR * b * SEQ // N_EXPERTS for b in batches]
    nbs = [b * SEQ // TB for b in batches]
    cps = [cap + PAD for cap in caps]
    cp_tot = sum(cps)
    layout = (nbs[0], cps[0], cp_tot)
    groups = ((0, nbs[0], caps[0]), (nbs[0], nbs[1], caps[1]))
    pad_rows = ((nbs[0] - 1, caps[0]), (nbs[0] + nbs[1] - 1, caps[1]))
    rows = N_EXPERTS * cp_tot
    nblk = n // TB

    cos_t, sa_t, sb_t = _rope_tables()
    ff = w_e_gate.shape[-1]
    w_in_b = _to_bf16(w_in, row_block=d // 4)
    w_eg_b = _to_bf16(w_e_gate.reshape(DEPTH * N_EXPERTS, d, ff))
    w_eu_b = _to_bf16(w_e_up.reshape(DEPTH * N_EXPERTS, d, ff))
    w_ed_b = _to_bf16(w_e_down.reshape(DEPTH * N_EXPERTS, ff, d))
    w_ba_b, w_bg_b, w_o_b = _to_bf16(w_br_attn), _to_bf16(w_br_gmlp), _to_bf16(w_out)
    w_sp_b = _to_bf16(w_spatial.reshape(DEPTH * GMLP_GROUPS, CHUNK, CHUNK))
    outs = None
    for l in range(DEPTH):
        lam_init = 0.8 - 0.6 * math.exp(-0.3 * l)
        mod3 = mod[l, :bt].reshape(bt * N_MOD, 1, d)
        proj = _inproj(xa, xb, n, n_a, xb_row0, mod3, norm1_g[l][None], w_in_b, l,
                       b_gate[l][None], cos_t, sa_t, sb_t)
        attn = _attention(proj.reshape(bt, SEQ, N_SEG * d), lam[l], subln_g[l][None], lam_init)
        wr = jnp.pad(jnp.concatenate([w_router[l], w_router[l]], axis=1),
                     ((0, 0), (0, LANES - 2 * N_EXPERTS)))
        wrh = wr.astype(BF16)
        wrl = (wr - wrh.astype(F32)).astype(BF16)
        x1, h2a, aff = _mix(proj, attn.reshape(n, d), xa, xb, n_a, xb_row0, mod3,
                            gmlp_ln_g[l][None], gmlp_ln_b[l][None],
                            w_sp_b[l * GMLP_GROUPS:(l + 1) * GMLP_GROUPS], b_spatial[l].T,
                            w_ba_b[l], w_bg_b[l], w_o_b[l], norm2_g[l][None], wrh, wrl)
        bits3 = lax.bitcast_convert_type(aff, I32).reshape(nblk, TB, N_EXPERTS).transpose(0, 2, 1)
        pos3, s0_t, cnt_t = _route(bits3, groups)
        plan = _window_plan(s0_t[:, :nblk].T.reshape(-1), cnt_t[:, :nblk].T.reshape(-1))
        xe = _dispatch(plan, h2a, pos3, layout, pad_rows, rows)
        ye = _ffn(xe, w_eg_b, w_eu_b, w_ed_b, cp_tot, l * N_EXPERTS)
        if l < DEPTH - 1:
            x = _combine(plan[0], plan[2], ye, pos3, x1, mod3, final_g[None], layout, False,
                         0, nblk)
            xa, xb, xb_row0 = x, x, n_a
        else:
            outs = tuple(
                _combine(plan[0], plan[2], ye, pos3, x1, mod3, final_g[None], layout, True,
                         fb, nb).reshape(bsz, SEQ, d)
                for fb, nb, bsz in ((0, nbs[0], batches[0]), (nbs[0], nbs[1], batches[1])))
    return outs


def kernel(x_prompt, x_sample, c_prompt, c_sample, norm1_g, w_ada, b_ada, w_in, b_gate, lam, subln_g, gmlp_ln_g, gmlp_ln_b, w_spatial, b_spatial, w_br_attn, w_br_gmlp, w_out, norm2_g, w_router, w_e_gate, w_e_up, w_e_down, final_g):
    return _trunk((x_prompt, x_sample), (c_prompt, c_sample), norm1_g, w_ada, b_ada, w_in,
                  b_gate, lam, subln_g, gmlp_ln_g, gmlp_ln_b, w_spatial, b_spatial, w_br_attn,
                  w_br_gmlp, w_out, norm2_g, w_router, w_e_gate, w_e_up, w_e_down, final_g)
```

```python
import functools
import math

import jax
import jax.numpy as jnp
from jax import lax
from jax.experimental import pallas as pl
from jax.experimental.pallas import tpu as pltpu

F32 = jnp.float32
BF16 = jnp.bfloat16
I32 = jnp.int32

D_MODEL = 1024
SEQ = 4096
DEPTH = 2
N_HEADS = 8
HEAD_DIM = 64
ROT_DIM = HEAD_DIM // 4
ROPE_THETA = 500000.0
CHUNK = 128
GMLP_GROUPS = 8
N_EXPERTS = 16
EXPERT_FF = 2048
CAPACITY_FACTOR = 2
N_MOD = 6
N_SEG = 7
EPS = 1e-6

LANES = 128
VMEM_LIMIT = 56 * 1024 * 1024

TM_PROJ = 512
NC_PROJ = 256
TQ = 256
NSUB_Q = 4
TK = 1024
SCORE_BOUND = 60.0
TM_MIX = 512
SUB_MIX = 256
TB = 256
SLAB = 64
ROWS_BF16 = 16
MAXCH = -(-(TB + ROWS_BF16 - 1) // SLAB)
PAD = TB


def _cparams(sem):
    return pltpu.CompilerParams(dimension_semantics=sem, vmem_limit_bytes=VMEM_LIMIT)


def _ada_kernel(c_ref, w_ref, b_ref, o_ref):
    c = c_ref[...]
    a = c * jax.nn.sigmoid(c)
    o_ref[0] = jnp.dot(a, w_ref[0], preferred_element_type=F32,
                       precision=lax.Precision.HIGHEST) + b_ref[0]


def _ada(c_pad, w_ada, b_ada):
    bp, d = c_pad.shape
    depth = w_ada.shape[0]
    return pl.pallas_call(
        _ada_kernel,
        grid=(depth, N_MOD),
        in_specs=[
            pl.BlockSpec((bp, d), lambda l, j: (0, 0)),
            pl.BlockSpec((1, d, d), lambda l, j: (l, 0, j)),
            pl.BlockSpec((1, 1, d), lambda l, j: (l, 0, j)),
        ],
        out_specs=pl.BlockSpec((1, bp, d), lambda l, j: (l, 0, j)),
        out_shape=jax.ShapeDtypeStruct((depth, bp, N_MOD * d), F32),
        compiler_params=_cparams(("arbitrary", "arbitrary")),
        name="ada_mod",
    )(c_pad, w_ada, b_ada.reshape(depth, 1, N_MOD * d))


def _gelu(x):
    return jax.nn.gelu(x)


def _cast_kernel(x_ref, o_ref):
    o_ref[...] = x_ref[...].astype(o_ref.dtype)


def _to_bf16(w, row_block=None):
    lead, r, c = w.shape
    rb = r if row_block is None else row_block
    spec = pl.BlockSpec((None, rb, c), lambda i, k: (i, k, 0))
    return pl.pallas_call(
        _cast_kernel,
        grid=(lead, r // rb),
        in_specs=[spec],
        out_specs=spec,
        out_shape=jax.ShapeDtypeStruct(w.shape, BF16),
        compiler_params=_cparams(("arbitrary", "arbitrary")),
        name="to_bf16",
    )(w)


def _inproj_kernel(xa_ref, xb_ref, sh_ref, sc_ref, g_ref, w_ref, bg_ref, cos_ref, sa_ref, sb_ref,
                   o_ref, *, tiles_a):
    x = jnp.where(pl.program_id(0) < tiles_a, xa_ref[...], xb_ref[...])
    r = x * lax.rsqrt(jnp.mean(x * x, axis=-1, keepdims=True) + EPS) * g_ref[...]
    h = (r * (1.0 + sc_ref[...]) + sh_ref[...]).astype(BF16)
    d = h.shape[1]
    half = ROT_DIM // 2
    q_scale = HEAD_DIM ** -0.5 * math.log2(math.e)
    rope_q = (cos_ref[...] * q_scale, sa_ref[...] * q_scale, sb_ref[...] * q_scale)
    rope_k = (cos_ref[...], sa_ref[...], sb_ref[...])

    def rope(acc, tables):
        cos, sa, sb = tables
        parts = []
        for c in range(acc.shape[1] // LANES):
            a = acc[:, c * LANES:(c + 1) * LANES]
            parts.append(a * cos + pltpu.roll(a, half, 1) * sa
                         + pltpu.roll(a, LANES - half, 1) * sb)
        return jnp.concatenate(parts, axis=-1)

    for c in range(w_ref.shape[1] // NC_PROJ):
        cols = slice(c * NC_PROJ, (c + 1) * NC_PROJ)
        seg = c * NC_PROJ // d
        acc = jnp.dot(h, w_ref[:, cols], preferred_element_type=F32)
        if seg == 0:
            out = rope(acc, rope_q)
        elif seg == 1:
            out = rope(acc, rope_k)
        elif seg == 2:
            out = acc
        elif seg in (3, 4):
            out = _gelu(acc)
        else:
            out = jax.nn.sigmoid(acc + bg_ref[:, c * NC_PROJ - 5 * d:(c + 1) * NC_PROJ - 5 * d])
        o_ref[:, cols] = out.astype(BF16)


def _two_group_specs(rows, d, tm, rows_a, xb_row0):
    ta, tb0 = rows_a // tm, xb_row0 // tm
    return ta, [pl.BlockSpec((tm, d), lambda i, *_: (jnp.minimum(i, ta - 1), 0)),
                pl.BlockSpec((tm, d), lambda i, *_: (jnp.maximum(i - ta, 0) + tb0, 0))]


def _inproj(xa, xb, n, rows_a, xb_row0, mod3, norm_g, w_in, layer, b_gate, cos_t, sa_t, sb_t):
    d = xa.shape[1]
    tm = min(TM_PROJ, SEQ)
    tiles_per_seq = SEQ // tm
    tiles_a, xspecs = _two_group_specs(n, d, tm, rows_a, xb_row0)
    return pl.pallas_call(
        functools.partial(_inproj_kernel, tiles_a=tiles_a),
        grid=(n // tm,),
        in_specs=xspecs + [
            pl.BlockSpec((None, 1, d), lambda i: ((i // tiles_per_seq) * N_MOD + 0, 0, 0)),
            pl.BlockSpec((None, 1, d), lambda i: ((i // tiles_per_seq) * N_MOD + 1, 0, 0)),
            pl.BlockSpec((1, d), lambda i: (0, 0)),
            pl.BlockSpec((None, d, N_SEG * d), lambda i: (layer, 0, 0),
                         pipeline_mode=pl.Buffered(1)),
            pl.BlockSpec((1, 2 * d), lambda i: (0, 0)),
            pl.BlockSpec((tm, LANES), lambda i: (i % tiles_per_seq, 0)),
            pl.BlockSpec((tm, LANES), lambda i: (i % tiles_per_seq, 0)),
            pl.BlockSpec((tm, LANES), lambda i: (i % tiles_per_seq, 0)),
        ],
        out_specs=pl.BlockSpec((tm, N_SEG * d), lambda i: (i, 0)),
        out_shape=jax.ShapeDtypeStruct((n, N_SEG * d), BF16),
        compiler_params=_cparams(("arbitrary",)),
        name="in_proj",
    )(xa, xb, mod3, mod3, norm_g, w_in, b_gate, cos_t, sa_t, sb_t)


def _attn_kernel(q_ref, k_ref, v_ref, lam_ref, sg_ref, o_ref, vaug_ref, p_ref, kn_ref, *,
                 lam_init):
    tqt, vd = q_ref.shape
    s_len = k_ref.shape[0]
    tk = min(TK, s_len)
    tq = min(TQ, tqt)
    nsub = tqt // tq
    nch = s_len // tk

    @pl.when(pl.program_id(2) == 0)
    def _():
        vaug_ref[:, :vd] = v_ref[...]
        vaug_ref[:, vd:] = jnp.ones((s_len, vd), BF16)
        kf = k_ref[...].astype(F32)
        kn_ref[0] = jnp.max(jnp.sum(kf * kf, axis=-1, keepdims=True))

    qf = q_ref[...].astype(F32)
    qn = jnp.max(jnp.sum(qf * qf, axis=-1, keepdims=True))
    dn = (((1,), (1,)), ((), ()))
    lp = lam_ref[...]
    lam = (jnp.exp(jnp.sum(lp[0:1] * lp[1:2], axis=-1, keepdims=True))
           - jnp.exp(jnp.sum(lp[2:3] * lp[3:4], axis=-1, keepdims=True)) + lam_init)

    def stacked(q):
        lane = lax.broadcasted_iota(I32, q.shape, 1)
        zero = jnp.zeros_like(q)
        return jnp.concatenate([jnp.where(lane < HEAD_DIM, q, zero),
                                jnp.where(lane >= HEAD_DIM, q, zero)], axis=0)

    def sub_ln(o):
        r = o * lax.rsqrt(jnp.mean(o * o, axis=-1, keepdims=True) + EPS)
        return ((r * sg_ref[...]) * (1.0 - lam_init)).astype(BF16)

    small = qn * kn_ref[0] <= SCORE_BOUND * SCORE_BOUND

    @pl.when(small)
    def _():
        prev = None
        for t in range(nsub + 1):
            if t < nsub:
                qq = stacked(q_ref[t * tq:(t + 1) * tq, :])
                l = jnp.zeros((2 * tq, 1), F32)
            if prev is not None:
                pt, pl_sum = prev
                a1 = lam * pl_sum[:tq] / pl_sum[tq:]
                acc = jnp.zeros((tq, vd), F32)
            for c in range(nch):
                ks = slice(c * tk, (c + 1) * tk)
                if t < nsub:
                    p = jnp.exp2(lax.dot_general(qq, k_ref[ks, :], dn, preferred_element_type=F32))
                    p_ref[t % 2, :, ks] = p
                    l = l + jnp.sum(p, axis=-1, keepdims=True)
                if prev is not None:
                    pd = p_ref[pt % 2, :tq, ks] - p_ref[pt % 2, tq:, ks] * a1
                    acc = acc + jnp.dot(pd.astype(BF16), v_ref[ks, :],
                                        preferred_element_type=F32)
            if prev is not None:
                o_ref[pt * tq:(pt + 1) * tq, :] = sub_ln(acc / pl_sum[:tq])
            prev = (t, l) if t < nsub else None

    @pl.when(jnp.logical_not(small))
    def _():
        def tile(t, carry):
            rows = pl.ds(pl.multiple_of(t * tq, tq), tq)
            qq = stacked(q_ref[rows, :])
            m = jnp.full((2 * tq, 1), -jnp.inf, F32)
            acc = jnp.zeros((2 * tq, 2 * vd), F32)
            for c in range(nch):
                ks = slice(c * tk, (c + 1) * tk)
                s = lax.dot_general(qq, k_ref[ks, :], dn, preferred_element_type=F32)
                m_new = jnp.maximum(m, jnp.max(s, axis=-1, keepdims=True))
                alpha = jnp.exp2(m - m_new)
                p = jnp.exp2(s - m_new).astype(BF16)
                acc = alpha * acc + jnp.dot(p, vaug_ref[ks, :], preferred_element_type=F32)
                m = m_new
            on = acc[:, :vd] / acc[:, vd:]
            o_ref[rows, :] = sub_ln(on[:tq] - lam * on[tq:])
            return carry

        lax.fori_loop(0, nsub, tile, 0)


def _attention(proj3, lam, subln_g, lam_init):
    bt, s, _ = proj3.shape
    vd = 2 * HEAD_DIM
    tq = min(TQ, s)
    tqt = min(TQ * NSUB_Q, s)
    return pl.pallas_call(
        functools.partial(_attn_kernel, lam_init=lam_init),
        grid=(bt, N_HEADS, s // tqt),
        in_specs=[
            pl.BlockSpec((None, tqt, vd), lambda b, h, i: (b, i, h)),
            pl.BlockSpec((None, s, vd), lambda b, h, i: (b, 0, N_HEADS + h)),
            pl.BlockSpec((None, s, vd), lambda b, h, i: (b, 0, 2 * N_HEADS + h)),
            pl.BlockSpec((4, HEAD_DIM), lambda b, h, i: (0, 0)),
            pl.BlockSpec((1, vd), lambda b, h, i: (0, 0)),
        ],
        out_specs=pl.BlockSpec((None, tqt, vd), lambda b, h, i: (b, i, h)),
        out_shape=jax.ShapeDtypeStruct((bt, s, N_HEADS * vd), BF16),
        scratch_shapes=[pltpu.VMEM((s, 2 * vd), BF16), pltpu.VMEM((2, 2 * tq, s), F32),
                        pltpu.SMEM((1,), F32)],
        compiler_params=_cparams(("arbitrary", "arbitrary", "arbitrary")),
        name="diff_attn",
    )(proj3, proj3, proj3, lam, subln_g)


def _mix_kernel(u_ref, vg_ref, ga_ref, gg_ref, at_ref, xa_ref, xb_ref, lng_ref, lnb_ref, ws_ref,
                bs_ref, wba_ref, wbg_ref, wo_ref, gt1_ref, n2g_ref, sh2_ref, sc2_ref, wrh_ref,
                wrl_ref, x1_ref, h2_ref, aff_ref, *, tiles_a):
    for sub in range(u_ref.shape[0] // SUB_MIX):
        _mix_rows(slice(sub * SUB_MIX, (sub + 1) * SUB_MIX), u_ref, vg_ref, ga_ref, gg_ref, at_ref,
                  xa_ref, xb_ref, lng_ref, lnb_ref, ws_ref, bs_ref, wba_ref, wbg_ref, wo_ref,
                  gt1_ref, n2g_ref, sh2_ref, sc2_ref, wrh_ref, wrl_ref, x1_ref, h2_ref, aff_ref,
                  tiles_a)


def _mix_rows(rs, u_ref, vg_ref, ga_ref, gg_ref, at_ref, xa_ref, xb_ref, lng_ref, lnb_ref, ws_ref,
              bs_ref, wba_ref, wbg_ref, wo_ref, gt1_ref, n2g_ref, sh2_ref, sc2_ref, wrh_ref,
              wrl_ref, x1_ref, h2_ref, aff_ref, tiles_a):
    tm = rs.stop - rs.start
    vg = vg_ref[rs, :].astype(F32)
    mu = jnp.mean(vg, axis=-1, keepdims=True)
    xc = vg - mu
    var = jnp.mean(xc * xc, axis=-1, keepdims=True)
    vn = (xc * lax.rsqrt(var + EPS) * lng_ref[...] + lnb_ref[...]).astype(BF16)
    bs = bs_ref[...]
    cols = []
    for g in range(GMLP_GROUPS):
        rows = []
        for c in range(tm // CHUNK):
            blk = vn[c * CHUNK:(c + 1) * CHUNK, g * LANES:(g + 1) * LANES]
            rows.append(jnp.dot(ws_ref[g], blk, preferred_element_type=F32) + bs[:, g:g + 1])
        cols.append(jnp.concatenate(rows, axis=0))
    mixed = jnp.concatenate(cols, axis=1)
    gm = (u_ref[rs, :].astype(F32) * mixed).astype(BF16)
    o_g = jnp.dot(gm, wbg_ref[...], preferred_element_type=F32)
    o_a = jnp.dot(at_ref[rs, :], wba_ref[...], preferred_element_type=F32)
    merged = (ga_ref[rs, :].astype(F32) * o_a + gg_ref[rs, :].astype(F32) * o_g).astype(BF16)
    mo = jnp.dot(merged, wo_ref[...], preferred_element_type=F32)
    x_in = jnp.where(pl.program_id(0) < tiles_a, xa_ref[rs, :], xb_ref[rs, :])
    x1 = x_in + gt1_ref[...] * mo
    x1_ref[rs, :] = x1
    r = x1 * lax.rsqrt(jnp.mean(x1 * x1, axis=-1, keepdims=True) + EPS) * n2g_ref[...]
    h2 = r * (1.0 + sc2_ref[...]) + sh2_ref[...]
    h2h = h2.astype(BF16)
    d = h2.shape[1]
    h2_ref[rs, :d] = h2h
    h2l = (h2 - h2h.astype(F32)).astype(BF16)
    wrh = wrh_ref[...]
    logits = (jnp.dot(h2h, wrh, preferred_element_type=F32)
              + jnp.dot(h2l, wrh, preferred_element_type=F32)
              + jnp.dot(h2h, wrl_ref[...], preferred_element_type=F32))
    lane = lax.broadcasted_iota(I32, logits.shape, 1)
    first = lane < N_EXPERTS
    m = jnp.max(jnp.where(first, logits, -jnp.inf), axis=-1, keepdims=True)
    p = jnp.exp(logits - m)
    aff = p / jnp.sum(jnp.where(first, p, 0.0), axis=-1, keepdims=True)
    aff_ref[rs, :] = aff[:, :N_EXPERTS]
    hi = aff.astype(BF16)
    lo = (aff - hi.astype(F32)).astype(BF16)
    h2_ref[rs, d:] = jnp.where(first, hi, jnp.where(lane < 2 * N_EXPERTS, lo, jnp.zeros_like(lo)))


def _mix(proj, attn, xa, xb, rows_a, xb_row0, mod3, lng, lnb, ws, bs_t, wba, wbg, wo, n2g, wrh,
         wrl):
    n, d = attn.shape
    tm = min(TM_MIX, SEQ)
    tps = SEQ // tm
    tiles_a, xspecs = _two_group_specs(n, d, tm, rows_a, xb_row0)
    row = lambda i: (i, 0)
    full2 = lambda i: (0, 0)

    def seg(k):
        return pl.BlockSpec((tm, d), lambda i: (i, k))

    def modspec(k):
        return pl.BlockSpec((None, 1, d), lambda i: ((i // tps) * N_MOD + k, 0, 0))

    return pl.pallas_call(
        functools.partial(_mix_kernel, tiles_a=tiles_a),
        grid=(n // tm,),
        in_specs=[
            seg(3), seg(4), seg(5), seg(6),
            pl.BlockSpec((tm, d), row),
        ] + xspecs + [
            pl.BlockSpec((1, d), full2),
            pl.BlockSpec((1, d), full2),
            pl.BlockSpec((GMLP_GROUPS, CHUNK, CHUNK), lambda i: (0, 0, 0)),
            pl.BlockSpec((CHUNK, GMLP_GROUPS), full2),
            pl.BlockSpec((d, d), full2),
            pl.BlockSpec((d, d), full2),
            pl.BlockSpec((d, d), full2),
            modspec(2),
            pl.BlockSpec((1, d), full2),
            modspec(3),
            modspec(4),
            pl.BlockSpec((d, LANES), full2),
            pl.BlockSpec((d, LANES), full2),
        ],
        out_specs=[
            pl.BlockSpec((tm, d), row),
            pl.BlockSpec((tm, d + LANES), row),
            pl.BlockSpec((tm, N_EXPERTS), row),
        ],
        out_shape=[
            jax.ShapeDtypeStruct((n, d), F32),
            jax.ShapeDtypeStruct((n, d + LANES), BF16),
            jax.ShapeDtypeStruct((n, N_EXPERTS), F32),
        ],
        compiler_params=_cparams(("arbitrary",)),
        name="branch_mix",
    )(proj, proj, proj, proj, attn, xa, xb, lng, lnb, ws, bs_t, wba, wbg, wo, mod3, n2g, mod3,
      mod3, wrh, wrl)


def _route_kernel(bits_ref, pos_ref, s0_ref, cnt_ref, *, groups):
    ne, tb = bits_ref.shape[1], bits_ref.shape[2]
    ri = lax.broadcasted_iota(I32, (tb, tb), 0)
    ci = lax.broadcasted_iota(I32, (tb, tb), 1)
    tri = jnp.where(ri < ci, 1.0, 0.0).astype(BF16)
    blk_lane = lax.broadcasted_iota(I32, s0_ref.shape, 1)
    s0_ref[...] = jnp.zeros(s0_ref.shape, I32)
    cnt_ref[...] = jnp.zeros(cnt_ref.shape, I32)

    for (b0, nb, cap) in groups:
        def count(pred_fn):
            def body(i, acc):
                return acc + jnp.where(pred_fn(bits_ref[b0 + i]), 1, 0)
            acc = lax.fori_loop(0, nb, body, jnp.zeros((ne, tb), I32), unroll=8)
            return jnp.sum(acc, axis=1, keepdims=True)

        def bs_body(k, thr):
            cand = thr | lax.shift_left(jnp.int32(1), jnp.int32(30) - k)
            c = count(lambda blk: blk >= cand)
            return jnp.where(c >= cap, cand, thr)

        thr = lax.fori_loop(0, 31, bs_body, jnp.zeros((ne, 1), I32))
        n_gt = count(lambda blk: blk > thr)
        need = (cap - n_gt).astype(F32)

        def scan_body(i, carry):
            ceq, cpos = carry
            blk = bits_ref[b0 + i]
            gt = blk > thr
            eq = blk == thr
            eqf = jnp.where(eq, 1.0, 0.0)
            eq_excl = jnp.dot(eqf.astype(BF16), tri, preferred_element_type=F32) + ceq
            self = jnp.where(gt, 1.0, jnp.where(eq_excl < need, eqf, 0.0))
            pos_excl = jnp.dot(self.astype(BF16), tri, preferred_element_type=F32) + cpos
            pos_ref[b0 + i] = jnp.where(self > 0.5, pos_excl.astype(I32), -1)
            n_sel = jnp.sum(self, axis=1, keepdims=True)
            hit = blk_lane == (b0 + i)
            s0_ref[...] = jnp.where(hit, cpos.astype(I32), s0_ref[...])
            cnt_ref[...] = jnp.where(hit, n_sel.astype(I32), cnt_ref[...])
            return (ceq + jnp.sum(eqf, axis=1, keepdims=True), cpos + n_sel)

        zero = jnp.zeros((ne, 1), F32)
        lax.fori_loop(0, nb, scan_body, (zero, zero))


def _route(bits3, groups):
    nblk, ne, tb = bits3.shape
    nbp = -(-nblk // LANES) * LANES
    return pl.pallas_call(
        functools.partial(_route_kernel, groups=groups),
        out_shape=[
            jax.ShapeDtypeStruct((nblk, ne, tb), I32),
            jax.ShapeDtypeStruct((ne, nbp), I32),
            jax.ShapeDtypeStruct((ne, nbp), I32),
        ],
        compiler_params=pltpu.CompilerParams(vmem_limit_bytes=VMEM_LIMIT),
        name="route_topc",
    )(bits3)


def _slot_base(b, e, layout):
    nb0, cp0, cp_tot = layout
    return e * cp_tot + jnp.where(b >= nb0, cp0, 0)


def _one_hot_rows(pos_ref, e, first_slot, tb):
    jrow = lax.broadcasted_iota(I32, (SLAB, tb), 0)
    return jnp.where(jrow == pos_ref[e:e + 1, :] - first_slot, 1.0, 0.0).astype(BF16)


def _one_hot_stack(pos_ref, first_slots, tb):
    return jnp.concatenate([_one_hot_rows(pos_ref, e, first_slots[e], tb)
                            for e in range(N_EXPERTS)], axis=0)


def _window_plan(s0, cnt):
    a0 = (s0 // ROWS_BF16) * ROWS_BF16
    r = s0 - a0
    nch = jnp.where(cnt > 0, (r + cnt + SLAB - 1) // SLAB, 0)
    nxt = ((s0 + cnt) // ROWS_BF16) * ROWS_BF16 - a0
    c1 = jnp.minimum(nxt // SLAB, jnp.maximum(nch - 1, 0))
    o1 = jnp.where(nxt // SLAB == c1, nxt % SLAB, SLAB - ROWS_BF16)
    return a0, r, nch, c1, o1


def _dispatch_kernel(a0_ref, r_ref, nch_ref, c1_ref, o1_ref, h_ref, pos_ref, xe_ref,
                     slab_ref, carry_ref, zero_ref, sems, *, layout, pad_rows):
    b = pl.program_id(0)
    tb, da = h_ref.shape

    @pl.when(b == 0)
    def _():
        carry_ref[...] = jnp.zeros(carry_ref.shape, BF16)

    h = h_ref[...]
    a0s = [a0_ref[b * N_EXPERTS + e] for e in range(N_EXPERTS)]
    nchs = [nch_ref[b * N_EXPERTS + e] for e in range(N_EXPERTS)]
    y0 = jnp.dot(_one_hot_stack(pos_ref, a0s, tb), h, preferred_element_type=F32).astype(BF16)
    row16 = lax.broadcasted_iota(I32, (ROWS_BF16, da), 0)

    def copy(e, c, blk=b):
        row = pl.multiple_of(_slot_base(blk, e, layout) + a0_ref[blk * N_EXPERTS + e] + c * SLAB,
                             ROWS_BF16)
        return pltpu.make_async_copy(slab_ref.at[e, c], xe_ref.at[pl.ds(row, SLAB)], sems.at[e, c])

    def drain_block(blk):
        for e in range(N_EXPERTS):
            def drain(c, carry, e=e):
                copy(e, c, blk).wait()
                return carry

            lax.fori_loop(0, nch_ref[blk * N_EXPERTS + e], drain, 0)

    prev_pending = b > 0
    for (last_b, _) in pad_rows:
        prev_pending = jnp.logical_and(prev_pending, b - 1 != last_b)

    @pl.when(prev_pending)
    def _():
        drain_block(b - 1)

    for e in range(N_EXPERTS):
        @pl.when(nchs[e] > 0)
        def _(e=e):
            ye = y0[e * SLAB:(e + 1) * SLAB]
            slab_ref[e, 0, :ROWS_BF16, :] = jnp.where(row16 < r_ref[b * N_EXPERTS + e],
                                                      carry_ref[e], ye[:ROWS_BF16])
            slab_ref[e, 0, ROWS_BF16:, :] = ye[ROWS_BF16:]
            copy(e, 0).start()

            def extra(c, carry):
                pt = _one_hot_rows(pos_ref, e, a0s[e] + c * SLAB, tb)
                slab_ref[e, c] = jnp.dot(pt, h, preferred_element_type=F32).astype(BF16)
                copy(e, c).start()
                return carry

            lax.fori_loop(1, nchs[e], extra, 0)
            o1 = pl.multiple_of(o1_ref[b * N_EXPERTS + e], ROWS_BF16)
            carry_ref[e] = slab_ref[e, c1_ref[b * N_EXPERTS + e], pl.ds(o1, ROWS_BF16), :]

    for (last_b, first_pad) in pad_rows:
        @pl.when(b == last_b)
        def _(last_b=last_b, first_pad=first_pad):
            drain_block(last_b)
            zero_ref[...] = jnp.zeros(zero_ref.shape, BF16)
            cps = []
            for e in range(N_EXPERTS):
                for c in range(PAD // SLAB):
                    r0 = _slot_base(last_b, e, layout) + first_pad + c * SLAB
                    cp = pltpu.make_async_copy(zero_ref, xe_ref.at[pl.ds(r0, SLAB)], sems.at[e, c])
                    cp.start()
                    cps.append(cp)
            for cp in cps:
                cp.wait()


def _dispatch(plan, h2a, pos3, layout, pad_rows, rows):
    n, da = h2a.shape
    grid_spec = pltpu.PrefetchScalarGridSpec(
        num_scalar_prefetch=5,
        grid=(n // TB,),
        in_specs=[
            pl.BlockSpec((TB, da), lambda b, *_: (b, 0)),
            pl.BlockSpec((None, N_EXPERTS, TB), lambda b, *_: (b, 0, 0)),
        ],
        out_specs=pl.BlockSpec(memory_space=pl.ANY),
        scratch_shapes=[
            pltpu.VMEM((N_EXPERTS, MAXCH, SLAB, da), BF16),
            pltpu.VMEM((N_EXPERTS, ROWS_BF16, da), BF16),
            pltpu.VMEM((SLAB, da), BF16),
            pltpu.SemaphoreType.DMA((N_EXPERTS, MAXCH)),
        ],
    )
    return pl.pallas_call(
        functools.partial(_dispatch_kernel, layout=layout, pad_rows=pad_rows),
        grid_spec=grid_spec,
        out_shape=jax.ShapeDtypeStruct((rows, da), BF16),
        compiler_params=_cparams(("arbitrary",)),
        name="moe_dispatch",
    )(*plan, h2a, pos3)


def _ffn_kernel(x_ref, wg_ref, wu_ref, wd_ref, y_ref):
    e = pl.program_id(0)
    d = wg_ref.shape[0]
    x = x_ref[:, :d]
    gl = x_ref[:, d:].astype(F32)
    lane = lax.broadcasted_iota(I32, gl.shape, 1)
    mine = jnp.logical_or(lane == e, lane == e + N_EXPERTS)
    gate = jnp.sum(jnp.where(mine, gl, 0.0), axis=-1, keepdims=True)
    g = jnp.dot(x, wg_ref[...], preferred_element_type=F32)
    u = jnp.dot(x, wu_ref[...], preferred_element_type=F32)
    hid = (g * jax.nn.sigmoid(g) * u).astype(BF16)
    y_ref[...] = (jnp.dot(hid, wd_ref[...], preferred_element_type=F32) * gate).astype(BF16)


def _ffn_tile(cp_tot):
    for tm in (512, 256, 128, 64):
        if cp_tot % tm == 0:
            return tm
    raise ValueError("slot rows per expert must be a multiple of 64")


def _ffn(xe, wg, wu, wd, cp_tot, e0):
    rows, da = xe.shape
    d, ff = wg.shape[1], wg.shape[2]
    tm = _ffn_tile(cp_tot)
    tiles = cp_tot // tm
    return pl.pallas_call(
        _ffn_kernel,
        grid=(N_EXPERTS, tiles),
        in_specs=[
            pl.BlockSpec((tm, da), lambda e, r: (e * tiles + r, 0)),
            pl.BlockSpec((None, d, ff), lambda e, r: (e + e0, 0, 0)),
            pl.BlockSpec((None, d, ff), lambda e, r: (e + e0, 0, 0)),
            pl.BlockSpec((None, ff, d), lambda e, r: (e + e0, 0, 0)),
        ],
        out_specs=pl.BlockSpec((tm, d), lambda e, r: (e * tiles + r, 0)),
        out_shape=jax.ShapeDtypeStruct((rows, d), BF16),
        compiler_params=_cparams(("arbitrary", "arbitrary")),
        name="expert_ffn",
    )(xe, wg, wu, wd)


def _combine_kernel(a0_ref, nch_ref, ye_ref, pos_ref, x_ref, gt_ref, fg_ref, o_ref,
                    win_ref, xwin_ref, acc_ref, sems, xsems, *, layout, final, first_block):
    i = pl.program_id(0)
    b = i + first_block
    slot = i % 2
    tb, d = x_ref.shape
    dn = (((0,), (0,)), ((), ()))
    a0s = [a0_ref[b * N_EXPERTS + e] for e in range(N_EXPERTS)]
    nchs = [nch_ref[b * N_EXPERTS + e] for e in range(N_EXPERTS)]

    def row_of(blk, e, c):
        return pl.multiple_of(_slot_base(blk, e, layout) + a0_ref[blk * N_EXPERTS + e] + c * SLAB,
                              ROWS_BF16)

    def first_copy(blk, e, buf):
        return pltpu.make_async_copy(ye_ref.at[pl.ds(row_of(blk, e, 0), SLAB)],
                                     win_ref.at[buf, pl.ds(e * SLAB, SLAB)], sems.at[buf, e])

    def extra_copy(e, c):
        return pltpu.make_async_copy(ye_ref.at[pl.ds(row_of(b, e, c), SLAB)], xwin_ref.at[e, c - 1],
                                     xsems.at[e, c - 1])

    @pl.when(i == 0)
    def _():
        for e in range(N_EXPERTS):
            first_copy(b, e, 0).start()

    @pl.when(i + 1 < pl.num_programs(0))
    def _():
        for e in range(N_EXPERTS):
            first_copy(b + 1, e, 1 - slot).start()

    for e in range(N_EXPERTS):
        def fetch(c, carry, e=e):
            extra_copy(e, c).start()
            return carry

        lax.fori_loop(1, nchs[e], fetch, 0)

    pt = _one_hot_stack(pos_ref, a0s, tb)
    for e in range(N_EXPERTS):
        first_copy(b, e, slot).wait()
    acc_ref[...] = lax.dot_general(pt, win_ref[slot], dn, preferred_element_type=F32)

    for e in range(N_EXPERTS):
        def extra(c, carry, e=e):
            extra_copy(e, c).wait()
            ptc = _one_hot_rows(pos_ref, e, a0s[e] + c * SLAB, tb)
            acc_ref[...] += lax.dot_general(ptc, xwin_ref[e, c - 1], dn, preferred_element_type=F32)
            return carry

        lax.fori_loop(1, nchs[e], extra, 0)

    x2 = x_ref[...] + gt_ref[...] * acc_ref[...]
    if final:
        x2 = x2 * lax.rsqrt(jnp.mean(x2 * x2, axis=-1, keepdims=True) + EPS) * fg_ref[...]
    o_ref[...] = x2


def _combine(a0, nch, ye, pos3, x1, mod3, final_g, layout, final, first_block, nblocks):
    d = x1.shape[1]
    tps = SEQ // TB
    fb = first_block
    grid_spec = pltpu.PrefetchScalarGridSpec(
        num_scalar_prefetch=2,
        grid=(nblocks,),
        in_specs=[
            pl.BlockSpec(memory_space=pl.ANY),
            pl.BlockSpec((None, N_EXPERTS, TB), lambda i, *_: (i + fb, 0, 0)),
            pl.BlockSpec((TB, d), lambda i, *_: (i + fb, 0)),
            pl.BlockSpec((None, 1, d), lambda i, *_: (((i + fb) // tps) * N_MOD + 5, 0, 0)),
            pl.BlockSpec((1, d), lambda i, *_: (0, 0)),
        ],
        out_specs=pl.BlockSpec((TB, d), lambda i, *_: (i, 0)),
        scratch_shapes=[
            pltpu.VMEM((2, N_EXPERTS * SLAB, d), BF16),
            pltpu.VMEM((N_EXPERTS, MAXCH - 1, SLAB, d), BF16),
            pltpu.VMEM((TB, d), F32),
            pltpu.SemaphoreType.DMA((2, N_EXPERTS)),
            pltpu.SemaphoreType.DMA((N_EXPERTS, MAXCH - 1)),
        ],
    )
    return pl.pallas_call(
        functools.partial(_combine_kernel, layout=layout, final=final, first_block=fb),
        grid_spec=grid_spec,
        out_shape=jax.ShapeDtypeStruct((nblocks * TB, d), F32),
        compiler_params=_cparams(("arbitrary",)),
        name="moe_combine",
    )(a0, nch, ye, pos3, x1, mod3, final_g)


def _rope_tables():
    half = ROT_DIM // 2
    inv = ROPE_THETA ** (-jnp.arange(0, ROT_DIM, 2, dtype=F32) / ROT_DIM)
    ang = jnp.arange(SEQ, dtype=F32)[:, None] * inv[None, :]
    cos, sin = jnp.cos(ang), jnp.sin(ang)
    dd = jnp.arange(LANES) % HEAD_DIM
    cos_l = jnp.where(dd[None, :] < ROT_DIM, cos[:, dd % half], 1.0)
    sin_l = sin[:, dd % half]
    sa = jnp.where((dd[None, :] >= half) & (dd[None, :] < ROT_DIM), sin_l, 0.0)
    sb = jnp.where(dd[None, :] < half, -sin_l, 0.0)
    return cos_l.astype(F32), sa.astype(F32), sb.astype(F32)


def _trunk(xs, cs, norm1_g, w_ada, b_ada, w_in, b_gate, lam, subln_g, gmlp_ln_g, gmlp_ln_b,
           w_spatial, b_spatial, w_br_attn, w_br_gmlp, w_out, norm2_g, w_router,
           w_e_gate, w_e_up, w_e_down, final_g):
    d = D_MODEL
    assert N_HEADS * 2 * HEAD_DIM == d and GMLP_GROUPS * LANES == d and CHUNK == LANES
    assert SEQ % TB == 0 and TB % CHUNK == 0
    batches = [x.shape[0] for x in xs]
    bt = sum(batches)
    n = bt * SEQ
    n_a = batches[0] * SEQ
    xa, xb, xb_row0 = xs[0].reshape(-1, d), xs[1].reshape(-1, d), 0
    c = jnp.concatenate(cs, axis=0)
    bp = -(-bt // 8) * 8
    c_pad = jnp.pad(c, ((0, bp - bt), (0, 0)))
    mod = _ada(c_pad, w_ada, b_ada)

    caps = [CAPACITY_FACTOR * b * SEQ // N_EXPERTS for b in batches]
    nbs = [b * SEQ // TB for b in batches]
    cps = [cap + PAD for cap in caps]
    cp_tot = sum(cps)
    layout = (nbs[0], cps[0], cp_tot)
    groups = ((0, nbs[0], caps[0]), (nbs[0], nbs[1], caps[1]))
    pad_rows = ((nbs[0] - 1, caps[0]), (nbs[0] + nbs[1] - 1, caps[1]))
    rows = N_EXPERTS * cp_tot
    nblk = n // TB

    cos_t, sa_t, sb_t = _rope_tables()
    ff = w_e_gate.shape[-1]
    w_in_b = _to_bf16(w_in, row_block=d // 4)
    w_eg_b = _to_bf16(w_e_gate.reshape(DEPTH * N_EXPERTS, d, ff))
    w_eu_b = _to_bf16(w_e_up.reshape(DEPTH * N_EXPERTS, d, ff))
    w_ed_b = _to_bf16(w_e_down.reshape(DEPTH * N_EXPERTS, ff, d))
    w_ba_b, w_bg_b, w_o_b = _to_bf16(w_br_attn), _to_bf16(w_br_gmlp), _to_bf16(w_out)
    w_sp_b = _to_bf16(w_spatial.reshape(DEPTH * GMLP_GROUPS, CHUNK, CHUNK))
    outs = None
    for l in range(DEPTH):
        lam_init = 0.8 - 0.6 * math.exp(-0.3 * l)
        mod3 = mod[l, :bt].reshape(bt * N_MOD, 1, d)
        proj = _inproj(xa, xb, n, n_a, xb_row0, mod3, norm1_g[l][None], w_in_b, l,
                       b_gate[l][None], cos_t, sa_t, sb_t)
        attn = _attention(proj.reshape(bt, SEQ, N_SEG * d), lam[l], subln_g[l][None], lam_init)
        wr = jnp.pad(jnp.concatenate([w_router[l], w_router[l]], axis=1),
                     ((0, 0), (0, LANES - 2 * N_EXPERTS)))
        wrh = wr.astype(BF16)
        wrl = (wr - wrh.astype(F32)).astype(BF16)
        x1, h2a, aff = _mix(proj, attn.reshape(n, d), xa, xb, n_a, xb_row0, mod3,
                            gmlp_ln_g[l][None], gmlp_ln_b[l][None],
                            w_sp_b[l * GMLP_GROUPS:(l + 1) * GMLP_GROUPS], b_spatial[l].T,
                            w_ba_b[l], w_bg_b[l], w_o_b[l], norm2_g[l][None], wrh, wrl)
        bits3 = lax.bitcast_convert_type(aff, I32).reshape(nblk, TB, N_EXPERTS).transpose(0, 2, 1)
        pos3, s0_t, cnt_t = _route(bits3, groups)
        plan = _window_plan(s0_t[:, :nblk].T.reshape(-1), cnt_t[:, :nblk].T.reshape(-1))
        xe = _dispatch(plan, h2a, pos3, layout, pad_rows, rows)
        ye = _ffn(xe, w_eg_b, w_eu_b, w_ed_b, cp_tot, l * N_EXPERTS)
        if l < DEPTH - 1:
            x = _combine(plan[0], plan[2], ye, pos3, x1, mod3, final_g[None], layout, False,
                         0, nblk)
            xa, xb, xb_row0 = x, x, n_a
        else:
            outs = tuple(
                _combine(plan[0], plan[2], ye, pos3, x1, mod3, final_g[None], layout, True,
                         fb, nb).reshape(bsz, SEQ, d)
                for fb, nb, bsz in ((0, nbs[0], batches[0]), (nbs[0], nbs[1], batches[1])))
    return outs


def kernel(x_prompt, x_sample, c_prompt, c_sample, norm1_g, w_ada, b_ada, w_in, b_gate, lam, subln_g, gmlp_ln_g, gmlp_ln_b, w_spatial, b_spatial, w_br_attn, w_br_gmlp, w_out, norm2_g, w_router, w_e_gate, w_e_up, w_e_down, final_g):
    return _trunk((x_prompt, x_sample), (c_prompt, c_sample), norm1_g, w_ada, b_ada, w_in,
                  b_gate, lam, subln_g, gmlp_ln_g, gmlp_ln_b, w_spatial, b_spatial, w_br_attn,
                  w_br_gmlp, w_out, norm2_g, w_router, w_e_gate, w_e_up, w_e_down, final_g)
```

```python
import functools
import math

import jax
import jax.numpy as jnp
from jax import lax
from jax.experimental import pallas as pl
from jax.experimental.pallas import tpu as pltpu

F32 = jnp.float32
BF16 = jnp.bfloat16
I32 = jnp.int32

D_MODEL = 1024
SEQ = 4096
DEPTH = 2
N_HEADS = 8
HEAD_DIM = 64
ROT_DIM = HEAD_DIM // 4
ROPE_THETA = 500000.0
CHUNK = 128
GMLP_GROUPS = 8
N_EXPERTS = 16
EXPERT_FF = 2048
CAPACITY_FACTOR = 2
N_MOD = 6
N_SEG = 7
EPS = 1e-6

LANES = 128
VMEM_LIMIT = 56 * 1024 * 1024

TM_PROJ = 512
NC_PROJ = 256
TQ = 256
NSUB_Q = 8
TK = 1024
SCORE_BOUND = 60.0
TM_MIX = 512
SUB_MIX = 256
TB = 256
SLAB = 64
ROWS_BF16 = 16
MAXCH = -(-(TB + ROWS_BF16 - 1) // SLAB)
PAD = TB


def _cparams(sem):
    return pltpu.CompilerParams(dimension_semantics=sem, vmem_limit_bytes=VMEM_LIMIT)


def _ada_kernel(c_ref, w_ref, b_ref, o_ref):
    c = c_ref[...]
    a = c * jax.nn.sigmoid(c)
    o_ref[0] = jnp.dot(a, w_ref[0], preferred_element_type=F32,
                       precision=lax.Precision.HIGHEST) + b_ref[0]


def _ada(c_pad, w_ada, b_ada):
    bp, d = c_pad.shape
    depth = w_ada.shape[0]
    return pl.pallas_call(
        _ada_kernel,
        grid=(depth, N_MOD),
        in_specs=[
            pl.BlockSpec((bp, d), lambda l, j: (0, 0)),
            pl.BlockSpec((1, d, d), lambda l, j: (l, 0, j)),
            pl.BlockSpec((1, 1, d), lambda l, j: (l, 0, j)),
        ],
        out_specs=pl.BlockSpec((1, bp, d), lambda l, j: (l, 0, j)),
        out_shape=jax.ShapeDtypeStruct((depth, bp, N_MOD * d), F32),
        compiler_params=_cparams(("arbitrary", "arbitrary")),
        name="ada_mod",
    )(c_pad, w_ada, b_ada.reshape(depth, 1, N_MOD * d))


def _gelu(x):
    return jax.nn.gelu(x)


def _cast_kernel(x_ref, o_ref):
    o_ref[...] = x_ref[...].astype(o_ref.dtype)


def _to_bf16(w, row_block=None):
    lead, r, c = w.shape
    rb = r if row_block is None else row_block
    spec = pl.BlockSpec((None, rb, c), lambda i, k: (i, k, 0))
    return pl.pallas_call(
        _cast_kernel,
        grid=(lead, r // rb),
        in_specs=[spec],
        out_specs=spec,
        out_shape=jax.ShapeDtypeStruct(w.shape, BF16),
        compiler_params=_cparams(("arbitrary", "arbitrary")),
        name="to_bf16",
    )(w)


def _inproj_kernel(xa_ref, xb_ref, sh_ref, sc_ref, g_ref, w_ref, bg_ref, cos_ref, sa_ref, sb_ref,
                   o_ref, *, tiles_a):
    x = jnp.where(pl.program_id(0) < tiles_a, xa_ref[...], xb_ref[...])
    r = x * lax.rsqrt(jnp.mean(x * x, axis=-1, keepdims=True) + EPS) * g_ref[...]
    h = (r * (1.0 + sc_ref[...]) + sh_ref[...]).astype(BF16)
    d = h.shape[1]
    half = ROT_DIM // 2
    q_scale = HEAD_DIM ** -0.5 * math.log2(math.e)
    rope_q = (cos_ref[...] * q_scale, sa_ref[...] * q_scale, sb_ref[...] * q_scale)
    rope_k = (cos_ref[...], sa_ref[...], sb_ref[...])

    def rope(acc, tables):
        cos, sa, sb = tables
        parts = []
        for c in range(acc.shape[1] // LANES):
            a = acc[:, c * LANES:(c + 1) * LANES]
            parts.append(a * cos + pltpu.roll(a, half, 1) * sa
                         + pltpu.roll(a, LANES - half, 1) * sb)
        return jnp.concatenate(parts, axis=-1)

    for c in range(w_ref.shape[1] // NC_PROJ):
        cols = slice(c * NC_PROJ, (c + 1) * NC_PROJ)
        seg = c * NC_PROJ // d
        acc = jnp.dot(h, w_ref[:, cols], preferred_element_type=F32)
        if seg == 0:
            out = rope(acc, rope_q)
        elif seg == 1:
            out = rope(acc, rope_k)
        elif seg == 2:
            out = acc
        elif seg in (3, 4):
            out = _gelu(acc)
        else:
            out = jax.nn.sigmoid(acc + bg_ref[:, c * NC_PROJ - 5 * d:(c + 1) * NC_PROJ - 5 * d])
        o_ref[:, cols] = out.astype(BF16)


def _two_group_specs(rows, d, tm, rows_a, xb_row0):
    ta, tb0 = rows_a // tm, xb_row0 // tm
    return ta, [pl.BlockSpec((tm, d), lambda i, *_: (jnp.minimum(i, ta - 1), 0)),
                pl.BlockSpec((tm, d), lambda i, *_: (jnp.maximum(i - ta, 0) + tb0, 0))]


def _inproj(xa, xb, n, rows_a, xb_row0, mod3, norm_g, w_in, layer, b_gate, cos_t, sa_t, sb_t):
    d = xa.shape[1]
    tm = min(TM_PROJ, SEQ)
    tiles_per_seq = SEQ // tm
    tiles_a, xspecs = _two_group_specs(n, d, tm, rows_a, xb_row0)
    return pl.pallas_call(
        functools.partial(_inproj_kernel, tiles_a=tiles_a),
        grid=(n // tm,),
        in_specs=xspecs + [
            pl.BlockSpec((None, 1, d), lambda i: ((i // tiles_per_seq) * N_MOD + 0, 0, 0)),
            pl.BlockSpec((None, 1, d), lambda i: ((i // tiles_per_seq) * N_MOD + 1, 0, 0)),
            pl.BlockSpec((1, d), lambda i: (0, 0)),
            pl.BlockSpec((None, d, N_SEG * d), lambda i: (layer, 0, 0),
                         pipeline_mode=pl.Buffered(1)),
            pl.BlockSpec((1, 2 * d), lambda i: (0, 0)),
            pl.BlockSpec((tm, LANES), lambda i: (i % tiles_per_seq, 0)),
            pl.BlockSpec((tm, LANES), lambda i: (i % tiles_per_seq, 0)),
            pl.BlockSpec((tm, LANES), lambda i: (i % tiles_per_seq, 0)),
        ],
        out_specs=pl.BlockSpec((tm, N_SEG * d), lambda i: (i, 0)),
        out_shape=jax.ShapeDtypeStruct((n, N_SEG * d), BF16),
        compiler_params=_cparams(("arbitrary",)),
        name="in_proj",
    )(xa, xb, mod3, mod3, norm_g, w_in, b_gate, cos_t, sa_t, sb_t)


def _attn_kernel(q_ref, k_ref, v_ref, lam_ref, sg_ref, o_ref, vaug_ref, p_ref, kn_ref, *,
                 lam_init):
    tqt, vd = q_ref.shape
    s_len = k_ref.shape[0]
    tk = min(TK, s_len)
    tq = min(TQ, tqt)
    nsub = tqt // tq
    nch = s_len // tk

    @pl.when(pl.program_id(2) == 0)
    def _():
        vaug_ref[:, :vd] = v_ref[...]
        vaug_ref[:, vd:] = jnp.ones((s_len, vd), BF16)
        kf = k_ref[...].astype(F32)
        kn_ref[0] = jnp.max(jnp.sum(kf * kf, axis=-1, keepdims=True))

    qf = q_ref[...].astype(F32)
    qn = jnp.max(jnp.sum(qf * qf, axis=-1, keepdims=True))
    dn = (((1,), (1,)), ((), ()))
    lp = lam_ref[...]
    lam = (jnp.exp(jnp.sum(lp[0:1] * lp[1:2], axis=-1, keepdims=True))
           - jnp.exp(jnp.sum(lp[2:3] * lp[3:4], axis=-1, keepdims=True)) + lam_init)

    def stacked(q):
        lane = lax.broadcasted_iota(I32, q.shape, 1)
        zero = jnp.zeros_like(q)
        return jnp.concatenate([jnp.where(lane < HEAD_DIM, q, zero),
                                jnp.where(lane >= HEAD_DIM, q, zero)], axis=0)

    def sub_ln(o):
        r = o * lax.rsqrt(jnp.mean(o * o, axis=-1, keepdims=True) + EPS)
        return ((r * sg_ref[...]) * (1.0 - lam_init)).astype(BF16)

    small = qn * kn_ref[0] <= SCORE_BOUND * SCORE_BOUND

    @pl.when(small)
    def _():
        prev = None
        for t in range(nsub + 1):
            if t < nsub:
                qq = stacked(q_ref[t * tq:(t + 1) * tq, :])
                l = jnp.zeros((2 * tq, 1), F32)
            if prev is not None:
                pt, pl_sum = prev
                a1 = lam * pl_sum[:tq] / pl_sum[tq:]
                acc = jnp.zeros((tq, vd), F32)
            for c in range(nch):
                ks = slice(c * tk, (c + 1) * tk)
                if t < nsub:
                    p = jnp.exp2(lax.dot_general(qq, k_ref[ks, :], dn, preferred_element_type=F32))
                    p_ref[t % 2, :, ks] = p
                    l = l + jnp.sum(p, axis=-1, keepdims=True)
                if prev is not None:
                    pd = p_ref[pt % 2, :tq, ks] - p_ref[pt % 2, tq:, ks] * a1
                    acc = acc + jnp.dot(pd.astype(BF16), v_ref[ks, :],
                                        preferred_element_type=F32)
            if prev is not None:
                o_ref[pt * tq:(pt + 1) * tq, :] = sub_ln(acc / pl_sum[:tq])
            prev = (t, l) if t < nsub else None

    @pl.when(jnp.logical_not(small))
    def _():
        def tile(t, carry):
            rows = pl.ds(pl.multiple_of(t * tq, tq), tq)
            qq = stacked(q_ref[rows, :])
            m = jnp.full((2 * tq, 1), -jnp.inf, F32)
            acc = jnp.zeros((2 * tq, 2 * vd), F32)
            for c in range(nch):
                ks = slice(c * tk, (c + 1) * tk)
                s = lax.dot_general(qq, k_ref[ks, :], dn, preferred_element_type=F32)
                m_new = jnp.maximum(m, jnp.max(s, axis=-1, keepdims=True))
                alpha = jnp.exp2(m - m_new)
                p = jnp.exp2(s - m_new).astype(BF16)
                acc = alpha * acc + jnp.dot(p, vaug_ref[ks, :], preferred_element_type=F32)
                m = m_new
            on = acc[:, :vd] / acc[:, vd:]
            o_ref[rows, :] = sub_ln(on[:tq] - lam * on[tq:])
            return carry

        lax.fori_loop(0, nsub, tile, 0)


def _attention(proj3, lam, subln_g, lam_init):
    bt, s, _ = proj3.shape
    vd = 2 * HEAD_DIM
    tq = min(TQ, s)
    tqt = min(TQ * NSUB_Q, s)
    return pl.pallas_call(
        functools.partial(_attn_kernel, lam_init=lam_init),
        grid=(bt, N_HEADS, s // tqt),
        in_specs=[
            pl.BlockSpec((None, tqt, vd), lambda b, h, i: (b, i, h)),
            pl.BlockSpec((None, s, vd), lambda b, h, i: (b, 0, N_HEADS + h)),
            pl.BlockSpec((None, s, vd), lambda b, h, i: (b, 0, 2 * N_HEADS + h)),
            pl.BlockSpec((4, HEAD_DIM), lambda b, h, i: (0, 0)),
            pl.BlockSpec((1, vd), lambda b, h, i: (0, 0)),
        ],
        out_specs=pl.BlockSpec((None, tqt, vd), lambda b, h, i: (b, i, h)),
        out_shape=jax.ShapeDtypeStruct((bt, s, N_HEADS * vd), BF16),
        scratch_shapes=[pltpu.VMEM((s, 2 * vd), BF16), pltpu.VMEM((2, 2 * tq, s), F32),
                        pltpu.SMEM((1,), F32)],
        compiler_params=_cparams(("arbitrary", "arbitrary", "arbitrary")),
        name="diff_attn",
    )(proj3, proj3, proj3, lam, subln_g)


def _mix_kernel(u_ref, vg_ref, ga_ref, gg_ref, at_ref, xa_ref, xb_ref, lng_ref, lnb_ref, ws_ref,
                bs_ref, wba_ref, wbg_ref, wo_ref, gt1_ref, n2g_ref, sh2_ref, sc2_ref, wrh_ref,
                wrl_ref, x1_ref, h2_ref, aff_ref, *, tiles_a):
    for sub in range(u_ref.shape[0] // SUB_MIX):
        _mix_rows(slice(sub * SUB_MIX, (sub + 1) * SUB_MIX), u_ref, vg_ref, ga_ref, gg_ref, at_ref,
                  xa_ref, xb_ref, lng_ref, lnb_ref, ws_ref, bs_ref, wba_ref, wbg_ref, wo_ref,
                  gt1_ref, n2g_ref, sh2_ref, sc2_ref, wrh_ref, wrl_ref, x1_ref, h2_ref, aff_ref,
                  tiles_a)


def _mix_rows(rs, u_ref, vg_ref, ga_ref, gg_ref, at_ref, xa_ref, xb_ref, lng_ref, lnb_ref, ws_ref,
              bs_ref, wba_ref, wbg_ref, wo_ref, gt1_ref, n2g_ref, sh2_ref, sc2_ref, wrh_ref,
              wrl_ref, x1_ref, h2_ref, aff_ref, tiles_a):
    tm = rs.stop - rs.start
    vg = vg_ref[rs, :].astype(F32)
    mu = jnp.mean(vg, axis=-1, keepdims=True)
    xc = vg - mu
    var = jnp.mean(xc * xc, axis=-1, keepdims=True)
    vn = (xc * lax.rsqrt(var + EPS) * lng_ref[...] + lnb_ref[...]).astype(BF16)
    bs = bs_ref[...]
    cols = []
    for g in range(GMLP_GROUPS):
        rows = []
        for c in range(tm // CHUNK):
            blk = vn[c * CHUNK:(c + 1) * CHUNK, g * LANES:(g + 1) * LANES]
            rows.append(jnp.dot(ws_ref[g], blk, preferred_element_type=F32) + bs[:, g:g + 1])
        cols.append(jnp.concatenate(rows, axis=0))
    mixed = jnp.concatenate(cols, axis=1)
    gm = (u_ref[rs, :].astype(F32) * mixed).astype(BF16)
    o_g = jnp.dot(gm, wbg_ref[...], preferred_element_type=F32)
    o_a = jnp.dot(at_ref[rs, :], wba_ref[...], preferred_element_type=F32)
    merged = (ga_ref[rs, :].astype(F32) * o_a + gg_ref[rs, :].astype(F32) * o_g).astype(BF16)
    mo = jnp.dot(merged, wo_ref[...], preferred_element_type=F32)
    x_in = jnp.where(pl.program_id(0) < tiles_a, xa_ref[rs, :], xb_ref[rs, :])
    x1 = x_in + gt1_ref[...] * mo
    x1_ref[rs, :] = x1
    r = x1 * lax.rsqrt(jnp.mean(x1 * x1, axis=-1, keepdims=True) + EPS) * n2g_ref[...]
    h2 = r * (1.0 + sc2_ref[...]) + sh2_ref[...]
    h2h = h2.astype(BF16)
    d = h2.shape[1]
    h2_ref[rs, :d] = h2h
    h2l = (h2 - h2h.astype(F32)).astype(BF16)
    wrh = wrh_ref[...]
    logits = (jnp.dot(h2h, wrh, preferred_element_type=F32)
              + jnp.dot(h2l, wrh, preferred_element_type=F32)
              + jnp.dot(h2h, wrl_ref[...], preferred_element_type=F32))
    lane = lax.broadcasted_iota(I32, logits.shape, 1)
    first = lane < N_EXPERTS
    m = jnp.max(jnp.where(first, logits, -jnp.inf), axis=-1, keepdims=True)
    p = jnp.exp(logits - m)
    aff = p / jnp.sum(jnp.where(first, p, 0.0), axis=-1, keepdims=True)
    aff_ref[rs, :] = aff[:, :N_EXPERTS]
    hi = aff.astype(BF16)
    lo = (aff - hi.astype(F32)).astype(BF16)
    h2_ref[rs, d:] = jnp.where(first, hi, jnp.where(lane < 2 * N_EXPERTS, lo, jnp.zeros_like(lo)))


def _mix(proj, attn, xa, xb, rows_a, xb_row0, mod3, lng, lnb, ws, bs_t, wba, wbg, wo, n2g, wrh,
         wrl):
    n, d = attn.shape
    tm = min(TM_MIX, SEQ)
    tps = SEQ // tm
    tiles_a, xspecs = _two_group_specs(n, d, tm, rows_a, xb_row0)
    row = lambda i: (i, 0)
    full2 = lambda i: (0, 0)

    def seg(k):
        return pl.BlockSpec((tm, d), lambda i: (i, k))

    def modspec(k):
        return pl.BlockSpec((None, 1, d), lambda i: ((i // tps) * N_MOD + k, 0, 0))

    return pl.pallas_call(
        functools.partial(_mix_kernel, tiles_a=tiles_a),
        grid=(n // tm,),
        in_specs=[
            seg(3), seg(4), seg(5), seg(6),
            pl.BlockSpec((tm, d), row),
        ] + xspecs + [
            pl.BlockSpec((1, d), full2),
            pl.BlockSpec((1, d), full2),
            pl.BlockSpec((GMLP_GROUPS, CHUNK, CHUNK), lambda i: (0, 0, 0)),
            pl.BlockSpec((CHUNK, GMLP_GROUPS), full2),
            pl.BlockSpec((d, d), full2),
            pl.BlockSpec((d, d), full2),
            pl.BlockSpec((d, d), full2),
            modspec(2),
            pl.BlockSpec((1, d), full2),
            modspec(3),
            modspec(4),
            pl.BlockSpec((d, LANES), full2),
            pl.BlockSpec((d, LANES), full2),
        ],
        out_specs=[
            pl.BlockSpec((tm, d), row),
            pl.BlockSpec((tm, d + LANES), row),
            pl.BlockSpec((tm, N_EXPERTS), row),
        ],
        out_shape=[
            jax.ShapeDtypeStruct((n, d), F32),
            jax.ShapeDtypeStruct((n, d + LANES), BF16),
            jax.ShapeDtypeStruct((n, N_EXPERTS), F32),
        ],
        compiler_params=_cparams(("arbitrary",)),
        name="branch_mix",
    )(proj, proj, proj, proj, attn, xa, xb, lng, lnb, ws, bs_t, wba, wbg, wo, mod3, n2g, mod3,
      mod3, wrh, wrl)


def _route_kernel(bits_ref, pos_ref, s0_ref, cnt_ref, *, groups):
    ne, tb = bits_ref.shape[1], bits_ref.shape[2]
    ri = lax.broadcasted_iota(I32, (tb, tb), 0)
    ci = lax.broadcasted_iota(I32, (tb, tb), 1)
    tri = jnp.where(ri < ci, 1.0, 0.0).astype(BF16)
    blk_lane = lax.broadcasted_iota(I32, s0_ref.shape, 1)
    s0_ref[...] = jnp.zeros(s0_ref.shape, I32)
    cnt_ref[...] = jnp.zeros(cnt_ref.shape, I32)

    for (b0, nb, cap) in groups:
        def count(pred_fn):
            def body(i, acc):
                return acc + jnp.where(pred_fn(bits_ref[b0 + i]), 1, 0)
            acc = lax.fori_loop(0, nb, body, jnp.zeros((ne, tb), I32), unroll=8)
            return jnp.sum(acc, axis=1, keepdims=True)

        def bs_body(k, thr):
            cand = thr | lax.shift_left(jnp.int32(1), jnp.int32(30) - k)
            c = count(lambda blk: blk >= cand)
            return jnp.where(c >= cap, cand, thr)

        thr = lax.fori_loop(0, 31, bs_body, jnp.zeros((ne, 1), I32))
        n_gt = count(lambda blk: blk > thr)
        need = (cap - n_gt).astype(F32)

        def scan_body(i, carry):
            ceq, cpos = carry
            blk = bits_ref[b0 + i]
            gt = blk > thr
            eq = blk == thr
            eqf = jnp.where(eq, 1.0, 0.0)
            eq_excl = jnp.dot(eqf.astype(BF16), tri, preferred_element_type=F32) + ceq
            self = jnp.where(gt, 1.0, jnp.where(eq_excl < need, eqf, 0.0))
            pos_excl = jnp.dot(self.astype(BF16), tri, preferred_element_type=F32) + cpos
            pos_ref[b0 + i] = jnp.where(self > 0.5, pos_excl.astype(I32), -1)
            n_sel = jnp.sum(self, axis=1, keepdims=True)
            hit = blk_lane == (b0 + i)
            s0_ref[...] = jnp.where(hit, cpos.astype(I32), s0_ref[...])
            cnt_ref[...] = jnp.where(hit, n_sel.astype(I32), cnt_ref[...])
            return (ceq + jnp.sum(eqf, axis=1, keepdims=True), cpos + n_sel)

        zero = jnp.zeros((ne, 1), F32)
        lax.fori_loop(0, nb, scan_body, (zero, zero))


def _route(bits3, groups):
    nblk, ne, tb = bits3.shape
    nbp = -(-nblk // LANES) * LANES
    return pl.pallas_call(
        functools.partial(_route_kernel, groups=groups),
        out_shape=[
            jax.ShapeDtypeStruct((nblk, ne, tb), I32),
            jax.ShapeDtypeStruct((ne, nbp), I32),
            jax.ShapeDtypeStruct((ne, nbp), I32),
        ],
        compiler_params=pltpu.CompilerParams(vmem_limit_bytes=VMEM_LIMIT),
        name="route_topc",
    )(bits3)


def _slot_base(b, e, layout):
    nb0, cp0, cp_tot = layout
    return e * cp_tot + jnp.where(b >= nb0, cp0, 0)


def _one_hot_rows(pos_ref, e, first_slot, tb):
    jrow = lax.broadcasted_iota(I32, (SLAB, tb), 0)
    return jnp.where(jrow == pos_ref[e:e + 1, :] - first_slot, 1.0, 0.0).astype(BF16)


def _one_hot_stack(pos_ref, first_slots, tb):
    return jnp.concatenate([_one_hot_rows(pos_ref, e, first_slots[e], tb)
                            for e in range(N_EXPERTS)], axis=0)


def _window_plan(s0, cnt):
    a0 = (s0 // ROWS_BF16) * ROWS_BF16
    r = s0 - a0
    nch = jnp.where(cnt > 0, (r + cnt + SLAB - 1) // SLAB, 0)
    nxt = ((s0 + cnt) // ROWS_BF16) * ROWS_BF16 - a0
    c1 = jnp.minimum(nxt // SLAB, jnp.maximum(nch - 1, 0))
    o1 = jnp.where(nxt // SLAB == c1, nxt % SLAB, SLAB - ROWS_BF16)
    return a0, r, nch, c1, o1


def _dispatch_kernel(a0_ref, r_ref, nch_ref, c1_ref, o1_ref, h_ref, pos_ref, xe_ref,
                     slab_ref, carry_ref, zero_ref, sems, *, layout, pad_rows):
    b = pl.program_id(0)
    tb, da = h_ref.shape

    @pl.when(b == 0)
    def _():
        carry_ref[...] = jnp.zeros(carry_ref.shape, BF16)

    h = h_ref[...]
    a0s = [a0_ref[b * N_EXPERTS + e] for e in range(N_EXPERTS)]
    nchs = [nch_ref[b * N_EXPERTS + e] for e in range(N_EXPERTS)]
    y0 = jnp.dot(_one_hot_stack(pos_ref, a0s, tb), h, preferred_element_type=F32).astype(BF16)
    row16 = lax.broadcasted_iota(I32, (ROWS_BF16, da), 0)

    def copy(e, c, blk=b):
        row = pl.multiple_of(_slot_base(blk, e, layout) + a0_ref[blk * N_EXPERTS + e] + c * SLAB,
                             ROWS_BF16)
        return pltpu.make_async_copy(slab_ref.at[e, c], xe_ref.at[pl.ds(row, SLAB)], sems.at[e, c])

    def drain_block(blk):
        for e in range(N_EXPERTS):
            def drain(c, carry, e=e):
                copy(e, c, blk).wait()
                return carry

            lax.fori_loop(0, nch_ref[blk * N_EXPERTS + e], drain, 0)

    prev_pending = b > 0
    for (last_b, _) in pad_rows:
        prev_pending = jnp.logical_and(prev_pending, b - 1 != last_b)

    @pl.when(prev_pending)
    def _():
        drain_block(b - 1)

    for e in range(N_EXPERTS):
        @pl.when(nchs[e] > 0)
        def _(e=e):
            ye = y0[e * SLAB:(e + 1) * SLAB]
            slab_ref[e, 0, :ROWS_BF16, :] = jnp.where(row16 < r_ref[b * N_EXPERTS + e],
                                                      carry_ref[e], ye[:ROWS_BF16])
            slab_ref[e, 0, ROWS_BF16:, :] = ye[ROWS_BF16:]
            copy(e, 0).start()

            def extra(c, carry):
                pt = _one_hot_rows(pos_ref, e, a0s[e] + c * SLAB, tb)
                slab_ref[e, c] = jnp.dot(pt, h, preferred_element_type=F32).astype(BF16)
                copy(e, c).start()
                return carry

            lax.fori_loop(1, nchs[e], extra, 0)
            o1 = pl.multiple_of(o1_ref[b * N_EXPERTS + e], ROWS_BF16)
            carry_ref[e] = slab_ref[e, c1_ref[b * N_EXPERTS + e], pl.ds(o1, ROWS_BF16), :]

    for (last_b, first_pad) in pad_rows:
        @pl.when(b == last_b)
        def _(last_b=last_b, first_pad=first_pad):
            drain_block(last_b)
            zero_ref[...] = jnp.zeros(zero_ref.shape, BF16)
            cps = []
            for e in range(N_EXPERTS):
                for c in range(PAD // SLAB):
                    r0 = _slot_base(last_b, e, layout) + first_pad + c * SLAB
                    cp = pltpu.make_async_copy(zero_ref, xe_ref.at[pl.ds(r0, SLAB)], sems.at[e, c])
                    cp.start()
                    cps.append(cp)
            for cp in cps:
                cp.wait()


def _dispatch(plan, h2a, pos3, layout, pad_rows, rows):
    n, da = h2a.shape
    grid_spec = pltpu.PrefetchScalarGridSpec(
        num_scalar_prefetch=5,
        grid=(n // TB,),
        in_specs=[
            pl.BlockSpec((TB, da), lambda b, *_: (b, 0)),
            pl.BlockSpec((None, N_EXPERTS, TB), lambda b, *_: (b, 0, 0)),
        ],
        out_specs=pl.BlockSpec(memory_space=pl.ANY),
        scratch_shapes=[
            pltpu.VMEM((N_EXPERTS, MAXCH, SLAB, da), BF16),
            pltpu.VMEM((N_EXPERTS, ROWS_BF16, da), BF16),
            pltpu.VMEM((SLAB, da), BF16),
            pltpu.SemaphoreType.DMA((N_EXPERTS, MAXCH)),
        ],
    )
    return pl.pallas_call(
        functools.partial(_dispatch_kernel, layout=layout, pad_rows=pad_rows),
        grid_spec=grid_spec,
        out_shape=jax.ShapeDtypeStruct((rows, da), BF16),
        compiler_params=_cparams(("arbitrary",)),
        name="moe_dispatch",
    )(*plan, h2a, pos3)


def _ffn_kernel(x_ref, wg_ref, wu_ref, wd_ref, y_ref):
    e = pl.program_id(0)
    d = wg_ref.shape[0]
    x = x_ref[:, :d]
    gl = x_ref[:, d:].astype(F32)
    lane = lax.broadcasted_iota(I32, gl.shape, 1)
    mine = jnp.logical_or(lane == e, lane == e + N_EXPERTS)
    gate = jnp.sum(jnp.where(mine, gl, 0.0), axis=-1, keepdims=True)
    g = jnp.dot(x, wg_ref[...], preferred_element_type=F32)
    u = jnp.dot(x, wu_ref[...], preferred_element_type=F32)
    hid = (g * jax.nn.sigmoid(g) * u).astype(BF16)
    y_ref[...] = (jnp.dot(hid, wd_ref[...], preferred_element_type=F32) * gate).astype(BF16)


def _ffn_tile(cp_tot):
    for tm in (512, 256, 128, 64):
        if cp_tot % tm == 0:
            return tm
    raise ValueError("slot rows per expert must be a multiple of 64")


def _ffn(xe, wg, wu, wd, cp_tot, e0):
    rows, da = xe.shape
    d, ff = wg.shape[1], wg.shape[2]
    tm = _ffn_tile(cp_tot)
    tiles = cp_tot // tm
    return pl.pallas_call(
        _ffn_kernel,
        grid=(N_EXPERTS, tiles),
        in_specs=[
            pl.BlockSpec((tm, da), lambda e, r: (e * tiles + r, 0)),
            pl.BlockSpec((None, d, ff), lambda e, r: (e + e0, 0, 0)),
            pl.BlockSpec((None, d, ff), lambda e, r: (e + e0, 0, 0)),
            pl.BlockSpec((None, ff, d), lambda e, r: (e + e0, 0, 0)),
        ],
        out_specs=pl.BlockSpec((tm, d), lambda e, r: (e * tiles + r, 0)),
        out_shape=jax.ShapeDtypeStruct((rows, d), BF16),
        compiler_params=_cparams(("arbitrary", "arbitrary")),
        name="expert_ffn",
    )(xe, wg, wu, wd)


def _combine_kernel(a0_ref, nch_ref, ye_ref, pos_ref, x_ref, gt_ref, fg_ref, o_ref,
                    win_ref, xwin_ref, acc_ref, sems, xsems, *, layout, final, first_block):
    i = pl.program_id(0)
    b = i + first_block
    slot = i % 2
    tb, d = x_ref.shape
    dn = (((0,), (0,)), ((), ()))
    a0s = [a0_ref[b * N_EXPERTS + e] for e in range(N_EXPERTS)]
    nchs = [nch_ref[b * N_EXPERTS + e] for e in range(N_EXPERTS)]

    def row_of(blk, e, c):
        return pl.multiple_of(_slot_base(blk, e, layout) + a0_ref[blk * N_EXPERTS + e] + c * SLAB,
                              ROWS_BF16)

    def first_copy(blk, e, buf):
        return pltpu.make_async_copy(ye_ref.at[pl.ds(row_of(blk, e, 0), SLAB)],
                                     win_ref.at[buf, pl.ds(e * SLAB, SLAB)], sems.at[buf, e])

    def extra_copy(e, c):
        return pltpu.make_async_copy(ye_ref.at[pl.ds(row_of(b, e, c), SLAB)], xwin_ref.at[e, c - 1],
                                     xsems.at[e, c - 1])

    @pl.when(i == 0)
    def _():
        for e in range(N_EXPERTS):
            first_copy(b, e, 0).start()

    @pl.when(i + 1 < pl.num_programs(0))
    def _():
        for e in range(N_EXPERTS):
            first_copy(b + 1, e, 1 - slot).start()

    for e in range(N_EXPERTS):
        def fetch(c, carry, e=e):
            extra_copy(e, c).start()
            return carry

        lax.fori_loop(1, nchs[e], fetch, 0)

    pt = _one_hot_stack(pos_ref, a0s, tb)
    for e in range(N_EXPERTS):
        first_copy(b, e, slot).wait()
    acc_ref[...] = lax.dot_general(pt, win_ref[slot], dn, preferred_element_type=F32)

    for e in range(N_EXPERTS):
        def extra(c, carry, e=e):
            extra_copy(e, c).wait()
            ptc = _one_hot_rows(pos_ref, e, a0s[e] + c * SLAB, tb)
            acc_ref[...] += lax.dot_general(ptc, xwin_ref[e, c - 1], dn, preferred_element_type=F32)
            return carry

        lax.fori_loop(1, nchs[e], extra, 0)

    x2 = x_ref[...] + gt_ref[...] * acc_ref[...]
    if final:
        x2 = x2 * lax.rsqrt(jnp.mean(x2 * x2, axis=-1, keepdims=True) + EPS) * fg_ref[...]
    o_ref[...] = x2


def _combine(a0, nch, ye, pos3, x1, mod3, final_g, layout, final, first_block, nblocks):
    d = x1.shape[1]
    tps = SEQ // TB
    fb = first_block
    grid_spec = pltpu.PrefetchScalarGridSpec(
        num_scalar_prefetch=2,
        grid=(nblocks,),
        in_specs=[
            pl.BlockSpec(memory_space=pl.ANY),
            pl.BlockSpec((None, N_EXPERTS, TB), lambda i, *_: (i + fb, 0, 0)),
            pl.BlockSpec((TB, d), lambda i, *_: (i + fb, 0)),
            pl.BlockSpec((None, 1, d), lambda i, *_: (((i + fb) // tps) * N_MOD + 5, 0, 0)),
            pl.BlockSpec((1, d), lambda i, *_: (0, 0)),
        ],
        out_specs=pl.BlockSpec((TB, d), lambda i, *_: (i, 0)),
        scratch_shapes=[
            pltpu.VMEM((2, N_EXPERTS * SLAB, d), BF16),
            pltpu.VMEM((N_EXPERTS, MAXCH - 1, SLAB, d), BF16),
            pltpu.VMEM((TB, d), F32),
            pltpu.SemaphoreType.DMA((2, N_EXPERTS)),
            pltpu.SemaphoreType.DMA((N_EXPERTS, MAXCH - 1)),
        ],
    )
    return pl.pallas_call(
        functools.partial(_combine_kernel, layout=layout, final=final, first_block=fb),
        grid_spec=grid_spec,
        out_shape=jax.ShapeDtypeStruct((nblocks * TB, d), F32),
        compiler_params=_cparams(("arbitrary",)),
        name="moe_combine",
    )(a0, nch, ye, pos3, x1, mod3, final_g)


def _rope_tables():
    half = ROT_DIM // 2
    inv = ROPE_THETA ** (-jnp.arange(0, ROT_DIM, 2, dtype=F32) / ROT_DIM)
    ang = jnp.arange(SEQ, dtype=F32)[:, None] * inv[None, :]
    cos, sin = jnp.cos(ang), jnp.sin(ang)
    dd = jnp.arange(LANES) % HEAD_DIM
    cos_l = jnp.where(dd[None, :] < ROT_DIM, cos[:, dd % half], 1.0)
    sin_l = sin[:, dd % half]
    sa = jnp.where((dd[None, :] >= half) & (dd[None, :] < ROT_DIM), sin_l, 0.0)
    sb = jnp.where(dd[None, :] < half, -sin_l, 0.0)
    return cos_l.astype(F32), sa.astype(F32), sb.astype(F32)


def _trunk(xs, cs, norm1_g, w_ada, b_ada, w_in, b_gate, lam, subln_g, gmlp_ln_g, gmlp_ln_b,
           w_spatial, b_spatial, w_br_attn, w_br_gmlp, w_out, norm2_g, w_router,
           w_e_gate, w_e_up, w_e_down, final_g):
    d = D_MODEL
    assert N_HEADS * 2 * HEAD_DIM == d and GMLP_GROUPS * LANES == d and CHUNK == LANES
    assert SEQ % TB == 0 and TB % CHUNK == 0
    batches = [x.shape[0] for x in xs]
    bt = sum(batches)
    n = bt * SEQ
    n_a = batches[0] * SEQ
    xa, xb, xb_row0 = xs[0].reshape(-1, d), xs[1].reshape(-1, d), 0
    c = jnp.concatenate(cs, axis=0)
    bp = -(-bt // 8) * 8
    c_pad = jnp.pad(c, ((0, bp - bt), (0, 0)))
    mod = _ada(c_pad, w_ada, b_ada)

    caps = [CAPACITY_FACTOR * b * SEQ // N_EXPERTS for b in batches]
    nbs = [b * SEQ // TB for b in batches]
    cps = [cap + PAD for cap in caps]
    cp_tot = sum(cps)
    layout = (nbs[0], cps[0], cp_tot)
    groups = ((0, nbs[0], caps[0]), (nbs[0], nbs[1], caps[1]))
    pad_rows = ((nbs[0] - 1, caps[0]), (nbs[0] + nbs[1] - 1, caps[1]))
    rows = N_EXPERTS * cp_tot
    nblk = n // TB

    cos_t, sa_t, sb_t = _rope_tables()
    ff = w_e_gate.shape[-1]
    w_in_b = _to_bf16(w_in, row_block=d // 4)
    w_eg_b = _to_bf16(w_e_gate.reshape(DEPTH * N_EXPERTS, d, ff))
    w_eu_b = _to_bf16(w_e_up.reshape(DEPTH * N_EXPERTS, d, ff))
    w_ed_b = _to_bf16(w_e_down.reshape(DEPTH * N_EXPERTS, ff, d))
    w_ba_b, w_bg_b, w_o_b = _to_bf16(w_br_attn), _to_bf16(w_br_gmlp), _to_bf16(w_out)
    w_sp_b = _to_bf16(w_spatial.reshape(DEPTH * GMLP_GROUPS, CHUNK, CHUNK))
    outs = None
    for l in range(DEPTH):
        lam_init = 0.8 - 0.6 * math.exp(-0.3 * l)
        mod3 = mod[l, :bt].reshape(bt * N_MOD, 1, d)
        proj = _inproj(xa, xb, n, n_a, xb_row0, mod3, norm1_g[l][None], w_in_b, l,
                       b_gate[l][None], cos_t, sa_t, sb_t)
        attn = _attention(proj.reshape(bt, SEQ, N_SEG * d), lam[l], subln_g[l][None], lam_init)
        wr = jnp.pad(jnp.concatenate([w_router[l], w_router[l]], axis=1),
                     ((0, 0), (0, LANES - 2 * N_EXPERTS)))
        wrh = wr.astype(BF16)
        wrl = (wr - wrh.astype(F32)).astype(BF16)
        x1, h2a, aff = _mix(proj, attn.reshape(n, d), xa, xb, n_a, xb_row0, mod3,
                            gmlp_ln_g[l][None], gmlp_ln_b[l][None],
                            w_sp_b[l * GMLP_GROUPS:(l + 1) * GMLP_GROUPS], b_spatial[l].T,
                            w_ba_b[l], w_bg_b[l], w_o_b[l], norm2_g[l][None], wrh, wrl)
        bits3 = lax.bitcast_convert_type(aff, I32).reshape(nblk, TB, N_EXPERTS).transpose(0, 2, 1)
        pos3, s0_t, cnt_t = _route(bits3, groups)
        plan = _window_plan(s0_t[:, :nblk].T.reshape(-1), cnt_t[:, :nblk].T.reshape(-1))
        xe = _dispatch(plan, h2a, pos3, layout, pad_rows, rows)
        ye = _ffn(xe, w_eg_b, w_eu_b, w_ed_b, cp_tot, l * N_EXPERTS)
        if l < DEPTH - 1:
            x = _combine(plan[0], plan[2], ye, pos3, x1, mod3, final_g[None], layout, False,
                         0, nblk)
            xa, xb, xb_row0 = x, x, n_a
        else:
            outs = tuple(
                _combine(plan[0], plan[2], ye, pos3, x1, mod3, final_g[None], layout, True,
                         fb, nb).reshape(bsz, SEQ, d)
                for fb, nb, bsz in ((0, nbs[0], batches[0]), (nbs[0], nbs[1], batches[1])))
    return outs


def kernel(x_prompt, x_sample, c_prompt, c_sample, norm1_g, w_ada, b_ada, w_in, b_gate, lam, subln_g, gmlp_ln_g, gmlp_ln_b, w_spatial, b_spatial, w_br_attn, w_br_gmlp, w_out, norm2_g, w_router, w_e_gate, w_e_up, w_e_down, final_g):
    return _trunk((x_prompt, x_sample), (c_prompt, c_sample), norm1_g, w_ada, b_ada, w_in,
                  b_gate, lam, subln_g, gmlp_ln_g, gmlp_ln_b, w_spatial, b_spatial, w_br_attn,
                  w_br_gmlp, w_out, norm2_g, w_router, w_e_gate, w_e_up, w_e_down, final_g)
```

```python
import functools
import math

import jax
import jax.numpy as jnp
from jax import lax
from jax.experimental import pallas as pl
from jax.experimental.pallas import tpu as pltpu

F32 = jnp.float32
BF16 = jnp.bfloat16
I32 = jnp.int32

D_MODEL = 1024
SEQ = 4096
DEPTH = 2
N_HEADS = 8
HEAD_DIM = 64
ROT_DIM = HEAD_DIM // 4
ROPE_THETA = 500000.0
CHUNK = 128
GMLP_GROUPS = 8
N_EXPERTS = 16
EXPERT_FF = 2048
CAPACITY_FACTOR = 2
N_MOD = 6
N_SEG = 7
EPS = 1e-6

LANES = 128
VMEM_LIMIT = 56 * 1024 * 1024

TM_PROJ = 512
SUB_PROJ = 256
NC_PROJ = 256
TQ = 256
NSUB_Q = 8
TK = 1024
SCORE_BOUND = 60.0
TM_MIX = 512
SUB_MIX = 256
TB = 256
SLAB = 64
ROWS_BF16 = 16
MAXCH = -(-(TB + ROWS_BF16 - 1) // SLAB)
PAD = TB


def _cparams(sem):
    return pltpu.CompilerParams(dimension_semantics=sem, vmem_limit_bytes=VMEM_LIMIT)


def _ada_kernel(c_ref, w_ref, b_ref, o_ref):
    c = c_ref[...]
    a = c * jax.nn.sigmoid(c)
    o_ref[0] = jnp.dot(a, w_ref[0], preferred_element_type=F32,
                       precision=lax.Precision.HIGHEST) + b_ref[0]


def _ada(c_pad, w_ada, b_ada):
    bp, d = c_pad.shape
    depth = w_ada.shape[0]
    return pl.pallas_call(
        _ada_kernel,
        grid=(depth, N_MOD),
        in_specs=[
            pl.BlockSpec((bp, d), lambda l, j: (0, 0)),
            pl.BlockSpec((1, d, d), lambda l, j: (l, 0, j)),
            pl.BlockSpec((1, 1, d), lambda l, j: (l, 0, j)),
        ],
        out_specs=pl.BlockSpec((1, bp, d), lambda l, j: (l, 0, j)),
        out_shape=jax.ShapeDtypeStruct((depth, bp, N_MOD * d), F32),
        compiler_params=_cparams(("arbitrary", "arbitrary")),
        name="ada_mod",
    )(c_pad, w_ada, b_ada.reshape(depth, 1, N_MOD * d))


def _gelu(x):
    return jax.nn.gelu(x)


def _cast_kernel(x_ref, o_ref):
    o_ref[...] = x_ref[...].astype(o_ref.dtype)


def _to_bf16(w, row_block=None):
    lead, r, c = w.shape
    rb = r if row_block is None else row_block
    spec = pl.BlockSpec((None, rb, c), lambda i, k: (i, k, 0))
    return pl.pallas_call(
        _cast_kernel,
        grid=(lead, r // rb),
        in_specs=[spec],
        out_specs=spec,
        out_shape=jax.ShapeDtypeStruct(w.shape, BF16),
        compiler_params=_cparams(("arbitrary", "arbitrary")),
        name="to_bf16",
    )(w)


def _inproj_kernel(xa_ref, xb_ref, sh_ref, sc_ref, g_ref, w_ref, bg_ref, cos_ref, sa_ref, sb_ref,
                   o_ref, *, tiles_a):
    for sub in range(xa_ref.shape[0] // SUB_PROJ):
        _inproj_rows(slice(sub * SUB_PROJ, (sub + 1) * SUB_PROJ), xa_ref, xb_ref, sh_ref, sc_ref,
                     g_ref, w_ref, bg_ref, cos_ref, sa_ref, sb_ref, o_ref, tiles_a)


def _inproj_rows(rs, xa_ref, xb_ref, sh_ref, sc_ref, g_ref, w_ref, bg_ref, cos_ref, sa_ref, sb_ref,
                 o_ref, tiles_a):
    x = jnp.where(pl.program_id(0) < tiles_a, xa_ref[rs, :], xb_ref[rs, :])
    r = x * lax.rsqrt(jnp.mean(x * x, axis=-1, keepdims=True) + EPS) * g_ref[...]
    h = (r * (1.0 + sc_ref[...]) + sh_ref[...]).astype(BF16)
    d = h.shape[1]
    half = ROT_DIM // 2
    q_scale = HEAD_DIM ** -0.5 * math.log2(math.e)
    rope_q = (cos_ref[rs, :] * q_scale, sa_ref[rs, :] * q_scale, sb_ref[rs, :] * q_scale)
    rope_k = (cos_ref[rs, :], sa_ref[rs, :], sb_ref[rs, :])

    def rope(acc, tables):
        cos, sa, sb = tables
        parts = []
        for c in range(acc.shape[1] // LANES):
            a = acc[:, c * LANES:(c + 1) * LANES]
            parts.append(a * cos + pltpu.roll(a, half, 1) * sa
                         + pltpu.roll(a, LANES - half, 1) * sb)
        return jnp.concatenate(parts, axis=-1)

    for c in range(w_ref.shape[1] // NC_PROJ):
        cols = slice(c * NC_PROJ, (c + 1) * NC_PROJ)
        seg = c * NC_PROJ // d
        acc = jnp.dot(h, w_ref[:, cols], preferred_element_type=F32)
        if seg == 0:
            out = rope(acc, rope_q)
        elif seg == 1:
            out = rope(acc, rope_k)
        elif seg == 2:
            out = acc
        elif seg in (3, 4):
            out = _gelu(acc)
        else:
            out = jax.nn.sigmoid(acc + bg_ref[:, c * NC_PROJ - 5 * d:(c + 1) * NC_PROJ - 5 * d])
        o_ref[rs, cols] = out.astype(BF16)


def _two_group_specs(rows, d, tm, rows_a, xb_row0):
    ta, tb0 = rows_a // tm, xb_row0 // tm
    return ta, [pl.BlockSpec((tm, d), lambda i, *_: (jnp.minimum(i, ta - 1), 0)),
                pl.BlockSpec((tm, d), lambda i, *_: (jnp.maximum(i - ta, 0) + tb0, 0))]


def _inproj(xa, xb, n, rows_a, xb_row0, mod3, norm_g, w_in, layer, b_gate, cos_t, sa_t, sb_t):
    d = xa.shape[1]
    tm = min(TM_PROJ, SEQ)
    tiles_per_seq = SEQ // tm
    tiles_a, xspecs = _two_group_specs(n, d, tm, rows_a, xb_row0)
    return pl.pallas_call(
        functools.partial(_inproj_kernel, tiles_a=tiles_a),
        grid=(n // tm,),
        in_specs=xspecs + [
            pl.BlockSpec((None, 1, d), lambda i: ((i // tiles_per_seq) * N_MOD + 0, 0, 0)),
            pl.BlockSpec((None, 1, d), lambda i: ((i // tiles_per_seq) * N_MOD + 1, 0, 0)),
            pl.BlockSpec((1, d), lambda i: (0, 0)),
            pl.BlockSpec((None, d, N_SEG * d), lambda i: (layer, 0, 0),
                         pipeline_mode=pl.Buffered(1)),
            pl.BlockSpec((1, 2 * d), lambda i: (0, 0)),
            pl.BlockSpec((tm, LANES), lambda i: (i % tiles_per_seq, 0)),
            pl.BlockSpec((tm, LANES), lambda i: (i % tiles_per_seq, 0)),
            pl.BlockSpec((tm, LANES), lambda i: (i % tiles_per_seq, 0)),
        ],
        out_specs=pl.BlockSpec((tm, N_SEG * d), lambda i: (i, 0)),
        out_shape=jax.ShapeDtypeStruct((n, N_SEG * d), BF16),
        compiler_params=_cparams(("arbitrary",)),
        name="in_proj",
    )(xa, xb, mod3, mod3, norm_g, w_in, b_gate, cos_t, sa_t, sb_t)


def _attn_kernel(q_ref, k_ref, v_ref, lam_ref, sg_ref, o_ref, vaug_ref, p_ref, kn_ref, *,
                 lam_init):
    tqt, vd = q_ref.shape
    s_len = k_ref.shape[0]
    tk = min(TK, s_len)
    tq = min(TQ, tqt)
    nsub = tqt // tq
    nch = s_len // tk

    @pl.when(pl.program_id(2) == 0)
    def _():
        vaug_ref[:, :vd] = v_ref[...]
        vaug_ref[:, vd:] = jnp.ones((s_len, vd), BF16)
        kf = k_ref[...].astype(F32)
        kn_ref[0] = jnp.max(jnp.sum(kf * kf, axis=-1, keepdims=True))

    qf = q_ref[...].astype(F32)
    qn = jnp.max(jnp.sum(qf * qf, axis=-1, keepdims=True))
    dn = (((1,), (1,)), ((), ()))
    lp = lam_ref[...]
    lam = (jnp.exp(jnp.sum(lp[0:1] * lp[1:2], axis=-1, keepdims=True))
           - jnp.exp(jnp.sum(lp[2:3] * lp[3:4], axis=-1, keepdims=True)) + lam_init)

    def stacked(q):
        lane = lax.broadcasted_iota(I32, q.shape, 1)
        zero = jnp.zeros_like(q)
        return jnp.concatenate([jnp.where(lane < HEAD_DIM, q, zero),
                                jnp.where(lane >= HEAD_DIM, q, zero)], axis=0)

    def sub_ln(o):
        r = o * lax.rsqrt(jnp.mean(o * o, axis=-1, keepdims=True) + EPS)
        return ((r * sg_ref[...]) * (1.0 - lam_init)).astype(BF16)

    small = qn * kn_ref[0] <= SCORE_BOUND * SCORE_BOUND

    @pl.when(small)
    def _():
        prev = None
        for t in range(nsub + 1):
            if t < nsub:
                qq = stacked(q_ref[t * tq:(t + 1) * tq, :])
                l = jnp.zeros((2 * tq, 1), F32)
            if prev is not None:
                pt, pl_sum = prev
                a1 = lam * pl_sum[:tq] / pl_sum[tq:]
                acc = jnp.zeros((tq, vd), F32)
            for c in range(nch):
                ks = slice(c * tk, (c + 1) * tk)
                if t < nsub:
                    p = jnp.exp2(lax.dot_general(qq, k_ref[ks, :], dn, preferred_element_type=F32))
                    p_ref[t % 2, :, ks] = p
                    l = l + jnp.sum(p, axis=-1, keepdims=True)
                if prev is not None:
                    pd = p_ref[pt % 2, :tq, ks] - p_ref[pt % 2, tq:, ks] * a1
                    acc = acc + jnp.dot(pd.astype(BF16), v_ref[ks, :],
                                        preferred_element_type=F32)
            if prev is not None:
                o_ref[pt * tq:(pt + 1) * tq, :] = sub_ln(acc / pl_sum[:tq])
            prev = (t, l) if t < nsub else None

    @pl.when(jnp.logical_not(small))
    def _():
        def tile(t, carry):
            rows = pl.ds(pl.multiple_of(t * tq, tq), tq)
            qq = stacked(q_ref[rows, :])
            m = jnp.full((2 * tq, 1), -jnp.inf, F32)
            acc = jnp.zeros((2 * tq, 2 * vd), F32)
            for c in range(nch):
                ks = slice(c * tk, (c + 1) * tk)
                s = lax.dot_general(qq, k_ref[ks, :], dn, preferred_element_type=F32)
                m_new = jnp.maximum(m, jnp.max(s, axis=-1, keepdims=True))
                alpha = jnp.exp2(m - m_new)
                p = jnp.exp2(s - m_new).astype(BF16)
                acc = alpha * acc + jnp.dot(p, vaug_ref[ks, :], preferred_element_type=F32)
                m = m_new
            on = acc[:, :vd] / acc[:, vd:]
            o_ref[rows, :] = sub_ln(on[:tq] - lam * on[tq:])
            return carry

        lax.fori_loop(0, nsub, tile, 0)


def _attention(proj3, lam, subln_g, lam_init):
    bt, s, _ = proj3.shape
    vd = 2 * HEAD_DIM
    tq = min(TQ, s)
    tqt = min(TQ * NSUB_Q, s)
    return pl.pallas_call(
        functools.partial(_attn_kernel, lam_init=lam_init),
        grid=(bt, N_HEADS, s // tqt),
        in_specs=[
            pl.BlockSpec((None, tqt, vd), lambda b, h, i: (b, i, h)),
            pl.BlockSpec((None, s, vd), lambda b, h, i: (b, 0, N_HEADS + h)),
            pl.BlockSpec((None, s, vd), lambda b, h, i: (b, 0, 2 * N_HEADS + h)),
            pl.BlockSpec((4, HEAD_DIM), lambda b, h, i: (0, 0)),
            pl.BlockSpec((1, vd), lambda b, h, i: (0, 0)),
        ],
        out_specs=pl.BlockSpec((None, tqt, vd), lambda b, h, i: (b, i, h)),
        out_shape=jax.ShapeDtypeStruct((bt, s, N_HEADS * vd), BF16),
        scratch_shapes=[pltpu.VMEM((s, 2 * vd), BF16), pltpu.VMEM((2, 2 * tq, s), F32),
                        pltpu.SMEM((1,), F32)],
        compiler_params=_cparams(("arbitrary", "arbitrary", "arbitrary")),
        name="diff_attn",
    )(proj3, proj3, proj3, lam, subln_g)


def _mix_kernel(u_ref, vg_ref, ga_ref, gg_ref, at_ref, xa_ref, xb_ref, lng_ref, lnb_ref, ws_ref,
                bs_ref, wba_ref, wbg_ref, wo_ref, gt1_ref, n2g_ref, sh2_ref, sc2_ref, wrh_ref,
                wrl_ref, x1_ref, h2_ref, aff_ref, *, tiles_a):
    for sub in range(u_ref.shape[0] // SUB_MIX):
        _mix_rows(slice(sub * SUB_MIX, (sub + 1) * SUB_MIX), u_ref, vg_ref, ga_ref, gg_ref, at_ref,
                  xa_ref, xb_ref, lng_ref, lnb_ref, ws_ref, bs_ref, wba_ref, wbg_ref, wo_ref,
                  gt1_ref, n2g_ref, sh2_ref, sc2_ref, wrh_ref, wrl_ref, x1_ref, h2_ref, aff_ref,
                  tiles_a)


def _mix_rows(rs, u_ref, vg_ref, ga_ref, gg_ref, at_ref, xa_ref, xb_ref, lng_ref, lnb_ref, ws_ref,
              bs_ref, wba_ref, wbg_ref, wo_ref, gt1_ref, n2g_ref, sh2_ref, sc2_ref, wrh_ref,
              wrl_ref, x1_ref, h2_ref, aff_ref, tiles_a):
    tm = rs.stop - rs.start
    vg = vg_ref[rs, :].astype(F32)
    mu = jnp.mean(vg, axis=-1, keepdims=True)
    xc = vg - mu
    var = jnp.mean(xc * xc, axis=-1, keepdims=True)
    vn = (xc * lax.rsqrt(var + EPS) * lng_ref[...] + lnb_ref[...]).astype(BF16)
    bs = bs_ref[...]
    cols = []
    for g in range(GMLP_GROUPS):
        rows = []
        for c in range(tm // CHUNK):
            blk = vn[c * CHUNK:(c + 1) * CHUNK, g * LANES:(g + 1) * LANES]
            rows.append(jnp.dot(ws_ref[g], blk, preferred_element_type=F32) + bs[:, g:g + 1])
        cols.append(jnp.concatenate(rows, axis=0))
    mixed = jnp.concatenate(cols, axis=1)
    gm = (u_ref[rs, :].astype(F32) * mixed).astype(BF16)
    o_g = jnp.dot(gm, wbg_ref[...], preferred_element_type=F32)
    o_a = jnp.dot(at_ref[rs, :], wba_ref[...], preferred_element_type=F32)
    merged = (ga_ref[rs, :].astype(F32) * o_a + gg_ref[rs, :].astype(F32) * o_g).astype(BF16)
    mo = jnp.dot(merged, wo_ref[...], preferred_element_type=F32)
    x_in = jnp.where(pl.program_id(0) < tiles_a, xa_ref[rs, :], xb_ref[rs, :])
    x1 = x_in + gt1_ref[...] * mo
    x1_ref[rs, :] = x1
    r = x1 * lax.rsqrt(jnp.mean(x1 * x1, axis=-1, keepdims=True) + EPS) * n2g_ref[...]
    h2 = r * (1.0 + sc2_ref[...]) + sh2_ref[...]
    h2h = h2.astype(BF16)
    d = h2.shape[1]
    h2_ref[rs, :d] = h2h
    h2l = (h2 - h2h.astype(F32)).astype(BF16)
    wrh = wrh_ref[...]
    logits = (jnp.dot(h2h, wrh, preferred_element_type=F32)
              + jnp.dot(h2l, wrh, preferred_element_type=F32)
              + jnp.dot(h2h, wrl_ref[...], preferred_element_type=F32))
    lane = lax.broadcasted_iota(I32, logits.shape, 1)
    first = lane < N_EXPERTS
    m = jnp.max(jnp.where(first, logits, -jnp.inf), axis=-1, keepdims=True)
    p = jnp.exp(logits - m)
    aff = p / jnp.sum(jnp.where(first, p, 0.0), axis=-1, keepdims=True)
    aff_ref[rs, :] = aff[:, :N_EXPERTS]
    hi = aff.astype(BF16)
    lo = (aff - hi.astype(F32)).astype(BF16)
    h2_ref[rs, d:] = jnp.where(first, hi, jnp.where(lane < 2 * N_EXPERTS, lo, jnp.zeros_like(lo)))


def _mix(proj, attn, xa, xb, rows_a, xb_row0, mod3, lng, lnb, ws, bs_t, wba, wbg, wo, n2g, wrh,
         wrl):
    n, d = attn.shape
    tm = min(TM_MIX, SEQ)
    tps = SEQ // tm
    tiles_a, xspecs = _two_group_specs(n, d, tm, rows_a, xb_row0)
    row = lambda i: (i, 0)
    full2 = lambda i: (0, 0)

    def seg(k):
        return pl.BlockSpec((tm, d), lambda i: (i, k))

    def modspec(k):
        return pl.BlockSpec((None, 1, d), lambda i: ((i // tps) * N_MOD + k, 0, 0))

    return pl.pallas_call(
        functools.partial(_mix_kernel, tiles_a=tiles_a),
        grid=(n // tm,),
        in_specs=[
            seg(3), seg(4), seg(5), seg(6),
            pl.BlockSpec((tm, d), row),
        ] + xspecs + [
            pl.BlockSpec((1, d), full2),
            pl.BlockSpec((1, d), full2),
            pl.BlockSpec((GMLP_GROUPS, CHUNK, CHUNK), lambda i: (0, 0, 0)),
            pl.BlockSpec((CHUNK, GMLP_GROUPS), full2),
            pl.BlockSpec((d, d), full2),
            pl.BlockSpec((d, d), full2),
            pl.BlockSpec((d, d), full2),
            modspec(2),
            pl.BlockSpec((1, d), full2),
            modspec(3),
            modspec(4),
            pl.BlockSpec((d, LANES), full2),
            pl.BlockSpec((d, LANES), full2),
        ],
        out_specs=[
            pl.BlockSpec((tm, d), row),
            pl.BlockSpec((tm, d + LANES), row),
            pl.BlockSpec((tm, N_EXPERTS), row),
        ],
        out_shape=[
            jax.ShapeDtypeStruct((n, d), F32),
            jax.ShapeDtypeStruct((n, d + LANES), BF16),
            jax.ShapeDtypeStruct((n, N_EXPERTS), F32),
        ],
        compiler_params=_cparams(("arbitrary",)),
        name="branch_mix",
    )(proj, proj, proj, proj, attn, xa, xb, lng, lnb, ws, bs_t, wba, wbg, wo, mod3, n2g, mod3,
      mod3, wrh, wrl)


def _route_kernel(bits_ref, pos_ref, s0_ref, cnt_ref, *, groups):
    ne, tb = bits_ref.shape[1], bits_ref.shape[2]
    ri = lax.broadcasted_iota(I32, (tb, tb), 0)
    ci = lax.broadcasted_iota(I32, (tb, tb), 1)
    tri = jnp.where(ri < ci, 1.0, 0.0).astype(BF16)
    blk_lane = lax.broadcasted_iota(I32, s0_ref.shape, 1)
    s0_ref[...] = jnp.zeros(s0_ref.shape, I32)
    cnt_ref[...] = jnp.zeros(cnt_ref.shape, I32)

    for (b0, nb, cap) in groups:
        def count(pred_fn):
            def body(i, acc):
                return acc + jnp.where(pred_fn(bits_ref[b0 + i]), 1, 0)
            acc = lax.fori_loop(0, nb, body, jnp.zeros((ne, tb), I32), unroll=8)
            return jnp.sum(acc, axis=1, keepdims=True)

        def bs_body(k, thr):
            cand = thr | lax.shift_left(jnp.int32(1), jnp.int32(30) - k)
            c = count(lambda blk: blk >= cand)
            return jnp.where(c >= cap, cand, thr)

        thr = lax.fori_loop(0, 31, bs_body, jnp.zeros((ne, 1), I32))
        n_gt = count(lambda blk: blk > thr)
        need = (cap - n_gt).astype(F32)

        def scan_body(i, carry):
            ceq, cpos = carry
            blk = bits_ref[b0 + i]
            gt = blk > thr
            eq = blk == thr
            eqf = jnp.where(eq, 1.0, 0.0)
            eq_excl = jnp.dot(eqf.astype(BF16), tri, preferred_element_type=F32) + ceq
            self = jnp.where(gt, 1.0, jnp.where(eq_excl < need, eqf, 0.0))
            pos_excl = jnp.dot(self.astype(BF16), tri, preferred_element_type=F32) + cpos
            pos_ref[b0 + i] = jnp.where(self > 0.5, pos_excl.astype(I32), -1)
            n_sel = jnp.sum(self, axis=1, keepdims=True)
            hit = blk_lane == (b0 + i)
            s0_ref[...] = jnp.where(hit, cpos.astype(I32), s0_ref[...])
            cnt_ref[...] = jnp.where(hit, n_sel.astype(I32), cnt_ref[...])
            return (ceq + jnp.sum(eqf, axis=1, keepdims=True), cpos + n_sel)

        zero = jnp.zeros((ne, 1), F32)
        lax.fori_loop(0, nb, scan_body, (zero, zero))


def _route(bits3, groups):
    nblk, ne, tb = bits3.shape
    nbp = -(-nblk // LANES) * LANES
    return pl.pallas_call(
        functools.partial(_route_kernel, groups=groups),
        out_shape=[
            jax.ShapeDtypeStruct((nblk, ne, tb), I32),
            jax.ShapeDtypeStruct((ne, nbp), I32),
            jax.ShapeDtypeStruct((ne, nbp), I32),
        ],
        compiler_params=pltpu.CompilerParams(vmem_limit_bytes=VMEM_LIMIT),
        name="route_topc",
    )(bits3)


def _slot_base(b, e, layout):
    nb0, cp0, cp_tot = layout
    return e * cp_tot + jnp.where(b >= nb0, cp0, 0)


def _one_hot_rows(pos_ref, e, first_slot, tb):
    jrow = lax.broadcasted_iota(I32, (SLAB, tb), 0)
    return jnp.where(jrow == pos_ref[e:e + 1, :] - first_slot, 1.0, 0.0).astype(BF16)


def _one_hot_stack(pos_ref, first_slots, tb):
    return jnp.concatenate([_one_hot_rows(pos_ref, e, first_slots[e], tb)
                            for e in range(N_EXPERTS)], axis=0)


def _window_plan(s0, cnt):
    a0 = (s0 // ROWS_BF16) * ROWS_BF16
    r = s0 - a0
    nch = jnp.where(cnt > 0, (r + cnt + SLAB - 1) // SLAB, 0)
    nxt = ((s0 + cnt) // ROWS_BF16) * ROWS_BF16 - a0
    c1 = jnp.minimum(nxt // SLAB, jnp.maximum(nch - 1, 0))
    o1 = jnp.where(nxt // SLAB == c1, nxt % SLAB, SLAB - ROWS_BF16)
    return a0, r, nch, c1, o1


def _dispatch_kernel(a0_ref, r_ref, nch_ref, c1_ref, o1_ref, h_ref, pos_ref, xe_ref,
                     slab_ref, carry_ref, zero_ref, sems, *, layout, pad_rows):
    b = pl.program_id(0)
    tb, da = h_ref.shape

    @pl.when(b == 0)
    def _():
        carry_ref[...] = jnp.zeros(carry_ref.shape, BF16)

    h = h_ref[...]
    a0s = [a0_ref[b * N_EXPERTS + e] for e in range(N_EXPERTS)]
    nchs = [nch_ref[b * N_EXPERTS + e] for e in range(N_EXPERTS)]
    y0 = jnp.dot(_one_hot_stack(pos_ref, a0s, tb), h, preferred_element_type=F32).astype(BF16)
    row16 = lax.broadcasted_iota(I32, (ROWS_BF16, da), 0)

    def copy(e, c, blk=b):
        row = pl.multiple_of(_slot_base(blk, e, layout) + a0_ref[blk * N_EXPERTS + e] + c * SLAB,
                             ROWS_BF16)
        return pltpu.make_async_copy(slab_ref.at[e, c], xe_ref.at[pl.ds(row, SLAB)], sems.at[e, c])

    def drain_block(blk):
        for e in range(N_EXPERTS):
            def drain(c, carry, e=e):
                copy(e, c, blk).wait()
                return carry

            lax.fori_loop(0, nch_ref[blk * N_EXPERTS + e], drain, 0)

    prev_pending = b > 0
    for (last_b, _) in pad_rows:
        prev_pending = jnp.logical_and(prev_pending, b - 1 != last_b)

    @pl.when(prev_pending)
    def _():
        drain_block(b - 1)

    for e in range(N_EXPERTS):
        @pl.when(nchs[e] > 0)
        def _(e=e):
            ye = y0[e * SLAB:(e + 1) * SLAB]
            slab_ref[e, 0, :ROWS_BF16, :] = jnp.where(row16 < r_ref[b * N_EXPERTS + e],
                                                      carry_ref[e], ye[:ROWS_BF16])
            slab_ref[e, 0, ROWS_BF16:, :] = ye[ROWS_BF16:]
            copy(e, 0).start()

            def extra(c, carry):
                pt = _one_hot_rows(pos_ref, e, a0s[e] + c * SLAB, tb)
                slab_ref[e, c] = jnp.dot(pt, h, preferred_element_type=F32).astype(BF16)
                copy(e, c).start()
                return carry

            lax.fori_loop(1, nchs[e], extra, 0)
            o1 = pl.multiple_of(o1_ref[b * N_EXPERTS + e], ROWS_BF16)
            carry_ref[e] = slab_ref[e, c1_ref[b * N_EXPERTS + e], pl.ds(o1, ROWS_BF16), :]

    for (last_b, first_pad) in pad_rows:
        @pl.when(b == last_b)
        def _(last_b=last_b, first_pad=first_pad):
            drain_block(last_b)
            zero_ref[...] = jnp.zeros(zero_ref.shape, BF16)
            cps = []
            for e in range(N_EXPERTS):
                for c in range(PAD // SLAB):
                    r0 = _slot_base(last_b, e, layout) + first_pad + c * SLAB
                    cp = pltpu.make_async_copy(zero_ref, xe_ref.at[pl.ds(r0, SLAB)], sems.at[e, c])
                    cp.start()
                    cps.append(cp)
            for cp in cps:
                cp.wait()


def _dispatch(plan, h2a, pos3, layout, pad_rows, rows):
    n, da = h2a.shape
    grid_spec = pltpu.PrefetchScalarGridSpec(
        num_scalar_prefetch=5,
        grid=(n // TB,),
        in_specs=[
            pl.BlockSpec((TB, da), lambda b, *_: (b, 0)),
            pl.BlockSpec((None, N_EXPERTS, TB), lambda b, *_: (b, 0, 0)),
        ],
        out_specs=pl.BlockSpec(memory_space=pl.ANY),
        scratch_shapes=[
            pltpu.VMEM((N_EXPERTS, MAXCH, SLAB, da), BF16),
            pltpu.VMEM((N_EXPERTS, ROWS_BF16, da), BF16),
            pltpu.VMEM((SLAB, da), BF16),
            pltpu.SemaphoreType.DMA((N_EXPERTS, MAXCH)),
        ],
    )
    return pl.pallas_call(
        functools.partial(_dispatch_kernel, layout=layout, pad_rows=pad_rows),
        grid_spec=grid_spec,
        out_shape=jax.ShapeDtypeStruct((rows, da), BF16),
        compiler_params=_cparams(("arbitrary",)),
        name="moe_dispatch",
    )(*plan, h2a, pos3)


def _ffn_kernel(x_ref, wg_ref, wu_ref, wd_ref, y_ref):
    e = pl.program_id(0)
    d = wg_ref.shape[0]
    x = x_ref[:, :d]
    gl = x_ref[:, d:].astype(F32)
    lane = lax.broadcasted_iota(I32, gl.shape, 1)
    mine = jnp.logical_or(lane == e, lane == e + N_EXPERTS)
    gate = jnp.sum(jnp.where(mine, gl, 0.0), axis=-1, keepdims=True)
    g = jnp.dot(x, wg_ref[...], preferred_element_type=F32)
    u = jnp.dot(x, wu_ref[...], preferred_element_type=F32)
    hid = (g * jax.nn.sigmoid(g) * u).astype(BF16)
    y_ref[...] = (jnp.dot(hid, wd_ref[...], preferred_element_type=F32) * gate).astype(BF16)


def _ffn_tile(cp_tot):
    for tm in (512, 256, 128, 64):
        if cp_tot % tm == 0:
            return tm
    raise ValueError("slot rows per expert must be a multiple of 64")


def _ffn(xe, wg, wu, wd, cp_tot, e0):
    rows, da = xe.shape
    d, ff = wg.shape[1], wg.shape[2]
    tm = _ffn_tile(cp_tot)
    tiles = cp_tot // tm
    return pl.pallas_call(
        _ffn_kernel,
        grid=(N_EXPERTS, tiles),
        in_specs=[
            pl.BlockSpec((tm, da), lambda e, r: (e * tiles + r, 0)),
            pl.BlockSpec((None, d, ff), lambda e, r: (e + e0, 0, 0)),
            pl.BlockSpec((None, d, ff), lambda e, r: (e + e0, 0, 0)),
            pl.BlockSpec((None, ff, d), lambda e, r: (e + e0, 0, 0)),
        ],
        out_specs=pl.BlockSpec((tm, d), lambda e, r: (e * tiles + r, 0)),
        out_shape=jax.ShapeDtypeStruct((rows, d), BF16),
        compiler_params=_cparams(("arbitrary", "arbitrary")),
        name="expert_ffn",
    )(xe, wg, wu, wd)


def _combine_kernel(a0_ref, nch_ref, ye_ref, pos_ref, x_ref, gt_ref, fg_ref, o_ref,
                    win_ref, xwin_ref, acc_ref, sems, xsems, *, layout, final, first_block):
    i = pl.program_id(0)
    b = i + first_block
    slot = i % 2
    tb, d = x_ref.shape
    dn = (((0,), (0,)), ((), ()))
    a0s = [a0_ref[b * N_EXPERTS + e] for e in range(N_EXPERTS)]
    nchs = [nch_ref[b * N_EXPERTS + e] for e in range(N_EXPERTS)]

    def row_of(blk, e, c):
        return pl.multiple_of(_slot_base(blk, e, layout) + a0_ref[blk * N_EXPERTS + e] + c * SLAB,
                              ROWS_BF16)

    def first_copy(blk, e, buf):
        return pltpu.make_async_copy(ye_ref.at[pl.ds(row_of(blk, e, 0), SLAB)],
                                     win_ref.at[buf, pl.ds(e * SLAB, SLAB)], sems.at[buf, e])

    def extra_copy(e, c):
        return pltpu.make_async_copy(ye_ref.at[pl.ds(row_of(b, e, c), SLAB)], xwin_ref.at[e, c - 1],
                                     xsems.at[e, c - 1])

    @pl.when(i == 0)
    def _():
        for e in range(N_EXPERTS):
            first_copy(b, e, 0).start()

    @pl.when(i + 1 < pl.num_programs(0))
    def _():
        for e in range(N_EXPERTS):
            first_copy(b + 1, e, 1 - slot).start()

    for e in range(N_EXPERTS):
        def fetch(c, carry, e=e):
            extra_copy(e, c).start()
            return carry

        lax.fori_loop(1, nchs[e], fetch, 0)

    pt = _one_hot_stack(pos_ref, a0s, tb)
    for e in range(N_EXPERTS):
        first_copy(b, e, slot).wait()
    acc_ref[...] = lax.dot_general(pt, win_ref[slot], dn, preferred_element_type=F32)

    for e in range(N_EXPERTS):
        def extra(c, carry, e=e):
            extra_copy(e, c).wait()
            ptc = _one_hot_rows(pos_ref, e, a0s[e] + c * SLAB, tb)
            acc_ref[...] += lax.dot_general(ptc, xwin_ref[e, c - 1], dn, preferred_element_type=F32)
            return carry

        lax.fori_loop(1, nchs[e], extra, 0)

    x2 = x_ref[...] + gt_ref[...] * acc_ref[...]
    if final:
        x2 = x2 * lax.rsqrt(jnp.mean(x2 * x2, axis=-1, keepdims=True) + EPS) * fg_ref[...]
    o_ref[...] = x2


def _combine(a0, nch, ye, pos3, x1, mod3, final_g, layout, final, first_block, nblocks):
    d = x1.shape[1]
    tps = SEQ // TB
    fb = first_block
    grid_spec = pltpu.PrefetchScalarGridSpec(
        num_scalar_prefetch=2,
        grid=(nblocks,),
        in_specs=[
            pl.BlockSpec(memory_space=pl.ANY),
            pl.BlockSpec((None, N_EXPERTS, TB), lambda i, *_: (i + fb, 0, 0)),
            pl.BlockSpec((TB, d), lambda i, *_: (i + fb, 0)),
            pl.BlockSpec((None, 1, d), lambda i, *_: (((i + fb) // tps) * N_MOD + 5, 0, 0)),
            pl.BlockSpec((1, d), lambda i, *_: (0, 0)),
        ],
        out_specs=pl.BlockSpec((TB, d), lambda i, *_: (i, 0)),
        scratch_shapes=[
            pltpu.VMEM((2, N_EXPERTS * SLAB, d), BF16),
            pltpu.VMEM((N_EXPERTS, MAXCH - 1, SLAB, d), BF16),
            pltpu.VMEM((TB, d), F32),
            pltpu.SemaphoreType.DMA((2, N_EXPERTS)),
            pltpu.SemaphoreType.DMA((N_EXPERTS, MAXCH - 1)),
        ],
    )
    return pl.pallas_call(
        functools.partial(_combine_kernel, layout=layout, final=final, first_block=fb),
        grid_spec=grid_spec,
        out_shape=jax.ShapeDtypeStruct((nblocks * TB, d), F32),
        compiler_params=_cparams(("arbitrary",)),
        name="moe_combine",
    )(a0, nch, ye, pos3, x1, mod3, final_g)


def _rope_tables():
    half = ROT_DIM // 2
    inv = ROPE_THETA ** (-jnp.arange(0, ROT_DIM, 2, dtype=F32) / ROT_DIM)
    ang = jnp.arange(SEQ, dtype=F32)[:, None] * inv[None, :]
    cos, sin = jnp.cos(ang), jnp.sin(ang)
    dd = jnp.arange(LANES) % HEAD_DIM
    cos_l = jnp.where(dd[None, :] < ROT_DIM, cos[:, dd % half], 1.0)
    sin_l = sin[:, dd % half]
    sa = jnp.where((dd[None, :] >= half) & (dd[None, :] < ROT_DIM), sin_l, 0.0)
    sb = jnp.where(dd[None, :] < half, -sin_l, 0.0)
    return cos_l.astype(F32), sa.astype(F32), sb.astype(F32)


def _trunk(xs, cs, norm1_g, w_ada, b_ada, w_in, b_gate, lam, subln_g, gmlp_ln_g, gmlp_ln_b,
           w_spatial, b_spatial, w_br_attn, w_br_gmlp, w_out, norm2_g, w_router,
           w_e_gate, w_e_up, w_e_down, final_g):
    d = D_MODEL
    assert N_HEADS * 2 * HEAD_DIM == d and GMLP_GROUPS * LANES == d and CHUNK == LANES
    assert SEQ % TB == 0 and TB % CHUNK == 0
    batches = [x.shape[0] for x in xs]
    bt = sum(batches)
    n = bt * SEQ
    n_a = batches[0] * SEQ
    xa, xb, xb_row0 = xs[0].reshape(-1, d), xs[1].reshape(-1, d), 0
    c = jnp.concatenate(cs, axis=0)
    bp = -(-bt // 8) * 8
    c_pad = jnp.pad(c, ((0, bp - bt), (0, 0)))
    mod = _ada(c_pad, w_ada, b_ada)

    caps = [CAPACITY_FACTOR * b * SEQ // N_EXPERTS for b in batches]
    nbs = [b * SEQ // TB for b in batches]
    cps = [cap + PAD for cap in caps]
    cp_tot = sum(cps)
    layout = (nbs[0], cps[0], cp_tot)
    groups = ((0, nbs[0], caps[0]), (nbs[0], nbs[1], caps[1]))
    pad_rows = ((nbs[0] - 1, caps[0]), (nbs[0] + nbs[1] - 1, caps[1]))
    rows = N_EXPERTS * cp_tot
    nblk = n // TB

    cos_t, sa_t, sb_t = _rope_tables()
    ff = w_e_gate.shape[-1]
    w_in_b = _to_bf16(w_in, row_block=d // 4)
    w_eg_b = _to_bf16(w_e_gate.reshape(DEPTH * N_EXPERTS, d, ff))
    w_eu_b = _to_bf16(w_e_up.reshape(DEPTH * N_EXPERTS, d, ff))
    w_ed_b = _to_bf16(w_e_down.reshape(DEPTH * N_EXPERTS, ff, d))
    w_ba_b, w_bg_b, w_o_b = _to_bf16(w_br_attn), _to_bf16(w_br_gmlp), _to_bf16(w_out)
    w_sp_b = _to_bf16(w_spatial.reshape(DEPTH * GMLP_GROUPS, CHUNK, CHUNK))
    outs = None
    for l in range(DEPTH):
        lam_init = 0.8 - 0.6 * math.exp(-0.3 * l)
        mod3 = mod[l, :bt].reshape(bt * N_MOD, 1, d)
        proj = _inproj(xa, xb, n, n_a, xb_row0, mod3, norm1_g[l][None], w_in_b, l,
                       b_gate[l][None], cos_t, sa_t, sb_t)
        attn = _attention(proj.reshape(bt, SEQ, N_SEG * d), lam[l], subln_g[l][None], lam_init)
        wr = jnp.pad(jnp.concatenate([w_router[l], w_router[l]], axis=1),
                     ((0, 0), (0, LANES - 2 * N_EXPERTS)))
        wrh = wr.astype(BF16)
        wrl = (wr - wrh.astype(F32)).astype(BF16)
        x1, h2a, aff = _mix(proj, attn.reshape(n, d), xa, xb, n_a, xb_row0, mod3,
                            gmlp_ln_g[l][None], gmlp_ln_b[l][None],
                            w_sp_b[l * GMLP_GROUPS:(l + 1) * GMLP_GROUPS], b_spatial[l].T,
                            w_ba_b[l], w_bg_b[l], w_o_b[l], norm2_g[l][None], wrh, wrl)
        bits3 = lax.bitcast_convert_type(aff, I32).reshape(nblk, TB, N_EXPERTS).transpose(0, 2, 1)
        pos3, s0_t, cnt_t = _route(bits3, groups)
        plan = _window_plan(s0_t[:, :nblk].T.reshape(-1), cnt_t[:, :nblk].T.reshape(-1))
        xe = _dispatch(plan, h2a, pos3, layout, pad_rows, rows)
        ye = _ffn(xe, w_eg_b, w_eu_b, w_ed_b, cp_tot, l * N_EXPERTS)
        if l < DEPTH - 1:
            x = _combine(plan[0], plan[2], ye, pos3, x1, mod3, final_g[None], layout, False,
                         0, nblk)
            xa, xb, xb_row0 = x, x, n_a
        else:
            outs = tuple(
                _combine(plan[0], plan[2], ye, pos3, x1, mod3, final_g[None], layout, True,
                         fb, nb).reshape(bsz, SEQ, d)
                for fb, nb, bsz in ((0, nbs[0], batches[0]), (nbs[0], nbs[1], batches[1])))
    return outs


def kernel(x_prompt, x_sample, c_prompt, c_sample, norm1_g, w_ada, b_ada, w_in, b_gate, lam, subln_g, gmlp_ln_g, gmlp_ln_b, w_spatial, b_spatial, w_br_attn, w_br_gmlp, w_out, norm2_g, w_router, w_e_gate, w_e_up, w_e_down, final_g):
    return _trunk((x_prompt, x_sample), (c_prompt, c_sample), norm1_g, w_ada, b_ada, w_in,
                  b_gate, lam, subln_g, gmlp_ln_g, gmlp_ln_b, w_spatial, b_spatial, w_br_attn,
                  w_br_gmlp, w_out, norm2_g, w_router, w_e_gate, w_e_up, w_e_down, final_g)
```
